```python
import math
import jax, jax.numpy as jnp
from jax import lax
import numpy as np

D_MODEL = 4096
BATCH = 1
SEQ = 8192
DEPTH = 2

N_MIXERS = 2
N_LAYERS_A = (DEPTH + 1) // 2
N_LAYERS_B = DEPTH // 2
DEEPNORM_ALPHA = (2.0 * DEPTH) ** 0.25
DEEPNORM_BETA = (8.0 * DEPTH) ** -0.25
MOD_INIT = 0.25

GDN_HEADS = D_MODEL // 128
GDN_DK = 128
GDN_DV = 128
GDN_QK_WIDTH = GDN_HEADS * GDN_DK
GDN_V_WIDTH = GDN_HEADS * GDN_DV
GDN_CONV = 4
GDN_CHUNK = 64
GDN_CONV_CH = 2 * GDN_QK_WIDTH + GDN_V_WIDTH
GDN_IN_WIDTH = GDN_CONV_CH + GDN_V_WIDTH + 2 * GDN_HEADS

MLA_HEADS = D_MODEL // 128
MLA_Q_RANK = 896
MLA_KV_RANK = 512
MLA_NOPE = 128
MLA_ROPE = 64
MLA_V = 128
MLA_QK = MLA_NOPE + MLA_ROPE
MLA_V_WIDTH = MLA_HEADS * MLA_V
MLA_IN_WIDTH = MLA_Q_RANK + MLA_KV_RANK + MLA_ROPE + MLA_V_WIDTH
ROPE_THETA = 10000.0
Q_BLOCK = 128

RMS_EPS = 1e-6
LN_EPS = 1e-5

kernel_name = "hybrid_gdn_mla_deepnorm_adaln"


def rms_norm(x, g, eps=RMS_EPS):
    xf = x.astype(jnp.float32)
    y = xf * lax.rsqrt(jnp.mean(xf * xf, axis=-1, keepdims=True) + eps)
    return (y * g.astype(jnp.float32)).astype(x.dtype)


def layer_norm(x, g, b, eps=LN_EPS):
    xf = x.astype(jnp.float32)
    mu = jnp.mean(xf, axis=-1, keepdims=True)
    var = jnp.mean(jnp.square(xf - mu), axis=-1, keepdims=True)
    y = (xf - mu) * lax.rsqrt(var + eps) * g.astype(jnp.float32) + b.astype(jnp.float32)
    return y.astype(x.dtype)


def l2_normalize(x, eps=RMS_EPS):
    xf = x.astype(jnp.float32)
    return xf * lax.rsqrt(jnp.sum(xf * xf, axis=-1, keepdims=True) + eps)


def causal_depthwise_conv(x, w):
    k_width, ch = w.shape
    return lax.conv_general_dilated(
        x, w.astype(x.dtype)[:, None, :], window_strides=(1,), padding=[(k_width - 1, 0)],
        dimension_numbers=("NWC", "WIO", "NWC"), feature_group_count=ch)


def gated_delta_rule_chunked(q, k, v, g, beta):
    bsz, seq, heads, dk = q.shape
    dv = v.shape[-1]
    c = GDN_CHUNK
    n = seq // c
    q = q * (dk ** -0.5)

    def chunks(t):
        return t.reshape(bsz, n, c, heads, -1).transpose(1, 0, 3, 2, 4)

    qc, kc, vc = chunks(q), chunks(k), chunks(v)
    gc = g.reshape(bsz, n, c, heads).transpose(1, 0, 3, 2)
    bc = beta.reshape(bsz, n, c, heads).transpose(1, 0, 3, 2)
    gcum = jnp.cumsum(gc, axis=-1)
    incl = jnp.tril(jnp.ones((c, c), dtype=bool))
    strict = jnp.tril(jnp.ones((c, c), dtype=bool), -1)
    diff = gcum[..., :, None] - gcum[..., None, :]
    decay_mat = jnp.where(incl, jnp.exp(jnp.where(incl, diff, 0.0)), 0.0)

    kb = kc * bc[..., None]
    a_mat = jnp.where(strict, jnp.einsum("nbhcd,nbhsd->nbhcs", kb, kc) * decay_mat, 0.0)
    rhs = jnp.concatenate([vc * bc[..., None], kb * jnp.exp(gcum)[..., None]], axis=-1)
    sol = lax.linalg.triangular_solve(a_mat + jnp.eye(c, dtype=a_mat.dtype), rhs,
                                      left_side=True, lower=True, unit_diagonal=True)
    u, w = sol[..., :dv], sol[..., dv:]
    qk = jnp.where(incl, jnp.einsum("nbhcd,nbhsd->nbhcs", qc, kc) * decay_mat, 0.0)
    q_dec = qc * jnp.exp(gcum)[..., None]
    k_dec = kc * jnp.exp(gcum[..., -1:] - gcum)[..., None]
    g_last = jnp.exp(gcum[..., -1])

    def step(state, inp):
        u_i, w_i, qk_i, qd_i, kd_i, gl_i = inp
        v_new = u_i - jnp.einsum("bhcd,bhde->bhce", w_i, state)
        o_i = jnp.einsum("bhcd,bhde->bhce", qd_i, state) + jnp.einsum("bhcs,bhse->bhce", qk_i, v_new)
        state = state * gl_i[..., None, None] + jnp.einsum("bhcd,bhce->bhde", kd_i, v_new)
        return state, o_i

    s0 = jnp.zeros((bsz, heads, dk, dv), jnp.float32)
    _, o = lax.scan(step, s0, (u, w, qk, q_dec, k_dec, g_last))
    return o.transpose(1, 0, 3, 2, 4).reshape(bsz, seq, heads, dv)


def gated_deltanet_branch(h, w_in, w_conv, a_log, dt_bias, norm_g, w_out):
    bsz, seq, _ = h.shape
    proj = h @ w_in
    qkv, z, b_raw, a_raw = jnp.split(
        proj, [GDN_CONV_CH, GDN_CONV_CH + GDN_V_WIDTH, GDN_CONV_CH + GDN_V_WIDTH + GDN_HEADS], axis=-1)
    qkv = jax.nn.silu(causal_depthwise_conv(qkv, w_conv))
    q, k, v = jnp.split(qkv, [GDN_QK_WIDTH, 2 * GDN_QK_WIDTH], axis=-1)
    q = l2_normalize(q.reshape(bsz, seq, GDN_HEADS, GDN_DK))
    k = l2_normalize(k.reshape(bsz, seq, GDN_HEADS, GDN_DK))
    v = v.reshape(bsz, seq, GDN_HEADS, GDN_DV).astype(jnp.float32)
    beta = jax.nn.sigmoid(b_raw.astype(jnp.float32))
    g = -jnp.exp(a_log.astype(jnp.float32)) * jax.nn.softplus(
        a_raw.astype(jnp.float32) + dt_bias.astype(jnp.float32))
    o = gated_delta_rule_chunked(q, k, v, g, beta)
    o = rms_norm(o, norm_g).astype(h.dtype).reshape(bsz, seq, GDN_V_WIDTH)
    return (o * jax.nn.silu(z)) @ w_out


def apply_rope(x, positions):
    half = x.shape[-1] // 2
    inv_freq = ROPE_THETA ** (-jnp.arange(0, half, dtype=jnp.float32) / half)
    ang = positions.astype(jnp.float32)[..., None] * inv_freq
    cos = jnp.cos(ang)[:, :, None, :]
    sin = jnp.sin(ang)[:, :, None, :]
    xf = x.astype(jnp.float32)
    x1, x2 = xf[..., :half], xf[..., half:]
    return jnp.concatenate([x1 * cos - x2 * sin, x2 * cos + x1 * sin], axis=-1).astype(x.dtype)


def causal_attention_blocked(q, k, v):
    bsz, seq, heads, dqk = q.shape
    dv = v.shape[-1]
    nb = seq // Q_BLOCK
    scale = dqk ** -0.5
    qb = q.reshape(bsz, nb, Q_BLOCK, heads, dqk).transpose(1, 0, 2, 3, 4)
    kpos = jnp.arange(seq)

    def one_block(args):
        q_blk, start = args
        s = jnp.einsum("bqhd,bkhd->bhqk", q_blk, k, preferred_element_type=jnp.float32) * scale
        qpos = start + jnp.arange(Q_BLOCK)
        s = jnp.where(kpos[None, :] <= qpos[:, None], s, -jnp.inf)
        p = jax.nn.softmax(s, axis=-1).astype(v.dtype)
        return jnp.einsum("bhqk,bkhd->bqhd", p, v)

    out = lax.map(one_block, (qb, jnp.arange(nb) * Q_BLOCK))
    return out.transpose(1, 0, 2, 3, 4).reshape(bsz, seq, heads, dv)


def mla_branch(h, positions, w_in, q_norm_g, w_qb, kv_norm_g, w_kvb, w_out):
    bsz, seq, _ = h.shape
    proj = h @ w_in
    cq, ckv, k_rope, z = jnp.split(
        proj, [MLA_Q_RANK, MLA_Q_RANK + MLA_KV_RANK, MLA_Q_RANK + MLA_KV_RANK + MLA_ROPE], axis=-1)
    q = (rms_norm(cq, q_norm_g) @ w_qb).reshape(bsz, seq, MLA_HEADS, MLA_QK)
    kv = (rms_norm(ckv, kv_norm_g) @ w_kvb).reshape(bsz, seq, MLA_HEADS, MLA_NOPE + MLA_V)
    q_nope, q_rope = q[..., :MLA_NOPE], q[..., MLA_NOPE:]
    k_nope, v = kv[..., :MLA_NOPE], kv[..., MLA_NOPE:]
    q_rope = apply_rope(q_rope, positions)
    k_rope = apply_rope(k_rope[:, :, None, :], positions)
    q = jnp.concatenate([q_nope, q_rope], axis=-1)
    k = jnp.concatenate([k_nope, jnp.broadcast_to(k_rope, (bsz, seq, MLA_HEADS, MLA_ROPE))], axis=-1)
    o = causal_attention_blocked(q, k, v).reshape(bsz, seq, MLA_V_WIDTH)
    return (o * jax.nn.silu(z)) @ w_out


def setup_inputs(seed: int = 0) -> dict:
    key = jax.random.key(seed)
    ks = jax.random.split(key, 24)
    f32 = jnp.float32
    nrm = lambda k, shape, s: jax.random.normal(k, shape, f32) * s
    x = jax.random.normal(ks[0], (BATCH, SEQ, D_MODEL), f32)
    c = jax.random.normal(ks[1], (BATCH, D_MODEL), f32)
    positions = jnp.broadcast_to(jnp.arange(SEQ, dtype=jnp.int32), (BATCH, SEQ))
    w_mod = nrm(ks[2], (DEPTH, D_MODEL, 3 * D_MODEL), MOD_INIT * D_MODEL ** -0.5)
    b_mod = nrm(ks[3], (DEPTH, 3 * D_MODEL), 0.01)
    ln_g = 1.0 + nrm(ks[4], (DEPTH, D_MODEL), 0.01)
    ln_b = nrm(ks[5], (DEPTH, D_MODEL), 0.01)
    a_w_in = nrm(ks[6], (N_LAYERS_A, D_MODEL, GDN_IN_WIDTH), D_MODEL ** -0.5)
    a_w_conv = nrm(ks[7], (N_LAYERS_A, GDN_CONV, GDN_CONV_CH), GDN_CONV ** -0.5)
    a_a_log = jnp.log(jax.random.uniform(ks[8], (N_LAYERS_A, GDN_HEADS), f32, 1.0, 16.0))
    dt = jnp.exp(jax.random.uniform(ks[9], (N_LAYERS_A, GDN_HEADS), f32, math.log(1e-3), math.log(1e-1)))
    a_dt_bias = dt + jnp.log(-jnp.expm1(-dt))
    a_norm_g = 1.0 + nrm(ks[10], (N_LAYERS_A, GDN_DV), 0.01)
    a_w_out = nrm(ks[11], (N_LAYERS_A, GDN_V_WIDTH, D_MODEL), DEEPNORM_BETA * GDN_V_WIDTH ** -0.5)
    b_w_in = nrm(ks[12], (N_LAYERS_B, D_MODEL, MLA_IN_WIDTH), D_MODEL ** -0.5)
    b_q_norm_g = 1.0 + nrm(ks[13], (N_LAYERS_B, MLA_Q_RANK), 0.01)
    b_w_qb = nrm(ks[14], (N_LAYERS_B, MLA_Q_RANK, MLA_HEADS * MLA_QK), MLA_Q_RANK ** -0.5)
    b_kv_norm_g = 1.0 + nrm(ks[15], (N_LAYERS_B, MLA_KV_RANK), 0.01)
    b_w_kvb = nrm(ks[16], (N_LAYERS_B, MLA_KV_RANK, MLA_HEADS * (MLA_NOPE + MLA_V)), MLA_KV_RANK ** -0.5)
    b_w_out = nrm(ks[17], (N_LAYERS_B, MLA_V_WIDTH, D_MODEL), DEEPNORM_BETA * MLA_V_WIDTH ** -0.5)
    return {"x": x, "c": c, "positions": positions, "w_mod": w_mod, "b_mod": b_mod,
            "ln_g": ln_g, "ln_b": ln_b, "a_w_in": a_w_in, "a_w_conv": a_w_conv,
            "a_a_log": a_a_log, "a_dt_bias": a_dt_bias, "a_norm_g": a_norm_g, "a_w_out": a_w_out,
            "b_w_in": b_w_in, "b_q_norm_g": b_q_norm_g, "b_w_qb": b_w_qb,
            "b_kv_norm_g": b_kv_norm_g, "b_w_kvb": b_w_kvb, "b_w_out": b_w_out}


def reference(x, c, positions, w_mod, b_mod, ln_g, ln_b, a_w_in, a_w_conv, a_a_log, a_dt_bias,
              a_norm_g, a_w_out, b_w_in, b_q_norm_g, b_w_qb, b_kv_norm_g, b_w_kvb, b_w_out):
    c_act = jax.nn.silu(c)
    for i in range(DEPTH):
        mod = c_act @ w_mod[i] + b_mod[i]
        shift, scale, gate = jnp.split(mod, 3, axis=-1)
        h = x * (1.0 + scale[:, None, :]) + shift[:, None, :]
        j = i // N_MIXERS
        if i % N_MIXERS == 0:
            y = gated_deltanet_branch(h, a_w_in[j], a_w_conv[j], a_a_log[j], a_dt_bias[j],
                                      a_norm_g[j], a_w_out[j])
        else:
            y = mla_branch(h, positions, b_w_in[j], b_q_norm_g[j], b_w_qb[j],
                           b_kv_norm_g[j], b_w_kvb[j], b_w_out[j])
        x = layer_norm(DEEPNORM_ALPHA * x + (1.0 + gate[:, None, :]) * y, ln_g[i], ln_b[i])
    return x
```

```python
import functools

import jax
import jax.numpy as jnp
from jax import lax
from jax.experimental import pallas as pl
from jax.experimental.pallas import tpu as pltpu

F32 = jnp.float32
BF16 = jnp.bfloat16

D_MODEL = 4096
DEPTH = 2
DEEPNORM_ALPHA = (2.0 * DEPTH) ** 0.25
RMS_EPS = 1e-6
LN_EPS = 1e-5

HEADS = 32
HEAD_DIM = 128
GDN_CONV = 4
GDN_CHUNK = 128
GDN_BLOCK = 1024
CONV_HALO = 8

MLA_Q_RANK = 896
MLA_KV_RANK = 512
MLA_ROPE = 64
MLA_QK = HEAD_DIM + MLA_ROPE
MLA_QK_PAD = 256
ROPE_THETA = 10000.0
ATT_TQ = 512
ATT_TK = 512

VMEM_LIMIT = 56 * 1024 * 1024


def _params(*sem):
    return pltpu.CompilerParams(dimension_semantics=sem, vmem_limit_bytes=VMEM_LIMIT)


def _silu(x):
    return x / (1.0 + jnp.exp(-x))


def _mod_kernel(c_ref, w_ref, b_ref, o_ref):
    tn = o_ref.shape[-1]
    rows = 128

    def body(kk, acc):
        r = pl.multiple_of(kk * rows, rows)
        c = c_ref[pl.ds(r, rows), :]
        a = _silu(c)
        p = w_ref[pl.ds(r, rows), :] * jnp.concatenate([a] * (tn // 128), axis=1)
        for t in range(rows // 8):
            acc = acc + p[t * 8:(t + 1) * 8, :]
        return acc

    acc = lax.fori_loop(0, c_ref.shape[0] // rows, body, jnp.zeros((8, tn), F32))
    o_ref[...] = jnp.sum(acc, axis=0, keepdims=True) + b_ref[...]


def _modulation(c, w_mod, b_mod, tn=512):
    depth, d, n = w_mod.shape
    c_rep = jnp.broadcast_to(c.reshape(d, 1), (d, 128))
    out = pl.pallas_call(
        _mod_kernel,
        grid=(depth, n // tn),
        in_specs=[pl.BlockSpec((d, 128), lambda l, j: (0, 0)),
                  pl.BlockSpec((None, d, tn), lambda l, j: (l, 0, j)),
                  pl.BlockSpec((None, 1, tn), lambda l, j: (l, 0, j))],
        out_specs=pl.BlockSpec((None, 1, tn), lambda l, j: (l, 0, j)),
        out_shape=jax.ShapeDtypeStruct((depth, 1, n), F32),
        compiler_params=_params("parallel", "parallel"),
        name="adaln_mod",
    )(c_rep, w_mod, b_mod.reshape(depth, 1, n))
    return out.reshape(depth, 3, 1, d)


def _modulate_kernel(x_ref, sc_ref, sh_ref, h_ref):
    h_ref[...] = (x_ref[...] * (1.0 + sc_ref[...]) + sh_ref[...]).astype(h_ref.dtype)


def _modulate(x, scale, shift, bm=512):
    s, d = x.shape
    row = pl.BlockSpec((bm, d), lambda i: (i, 0))
    vec = pl.BlockSpec((1, d), lambda i: (0, 0))
    return pl.pallas_call(
        _modulate_kernel, grid=(s // bm,), in_specs=[row, vec, vec], out_specs=row,
        out_shape=jax.ShapeDtypeStruct((s, d), BF16),
        compiler_params=_params("parallel"), name="modulate",
    )(x, scale, shift)


def _resid_ln(x_ref, y_ref, gate_ref, g_ref, b_ref):
    t = DEEPNORM_ALPHA * x_ref[...] + (1.0 + gate_ref[...]) * y_ref[...].astype(F32)
    mu = jnp.mean(t, axis=-1, keepdims=True)
    dlt = t - mu
    var = jnp.mean(dlt * dlt, axis=-1, keepdims=True)
    return dlt * lax.rsqrt(var + LN_EPS) * g_ref[...] + b_ref[...]


def _resid_ln_kernel(x_ref, y_ref, gate_ref, g_ref, b_ref, xo_ref):
    xo_ref[...] = _resid_ln(x_ref, y_ref, gate_ref, g_ref, b_ref)


def _resid_ln_mod_kernel(x_ref, y_ref, gate_ref, g_ref, b_ref, sc_ref, sh_ref, xo_ref, h_ref):
    xn = _resid_ln(x_ref, y_ref, gate_ref, g_ref, b_ref)
    xo_ref[...] = xn
    h_ref[...] = (xn * (1.0 + sc_ref[...]) + sh_ref[...]).astype(h_ref.dtype)


def _residual_layernorm(x, y, gate, ln_g, ln_b, next_scale=None, next_shift=None, bm=256):
    s, d = x.shape
    row = pl.BlockSpec((bm, d), lambda i: (i, 0))
    vec = pl.BlockSpec((1, d), lambda i: (0, 0))
    if next_scale is None:
        return pl.pallas_call(
            _resid_ln_kernel, grid=(s // bm,), in_specs=[row, row, vec, vec, vec], out_specs=row,
            out_shape=jax.ShapeDtypeStruct((s, d), F32),
            compiler_params=_params("parallel"), name="resid_ln",
        )(x, y, gate, ln_g, ln_b)
    return pl.pallas_call(
        _resid_ln_mod_kernel, grid=(s // bm,), in_specs=[row, row, vec, vec, vec, vec, vec],
        out_specs=[row, row],
        out_shape=[jax.ShapeDtypeStruct((s, d), F32), jax.ShapeDtypeStruct((s, d), BF16)],
        compiler_params=_params("parallel"), name="resid_ln_mod",
    )(x, y, gate, ln_g, ln_b, next_scale, next_shift)


def _mm_kernel(a_ref, b_ref, o_ref):
    o_ref[...] = jnp.dot(a_ref[...], b_ref[...], preferred_element_type=F32).astype(o_ref.dtype)


def _matmul(a, b, out_dtype, bm, bn):
    m, k = a.shape
    n = b.shape[1]
    bm = min(bm, m)
    return pl.pallas_call(
        _mm_kernel, grid=(n // bn, m // bm),
        in_specs=[pl.BlockSpec((bm, k), lambda j, i: (i, 0)),
                  pl.BlockSpec((k, bn), lambda j, i: (0, j))],
        out_specs=pl.BlockSpec((bm, bn), lambda j, i: (i, j)),
        out_shape=jax.ShapeDtypeStruct((m, n), out_dtype),
        compiler_params=_params("parallel", "parallel"), name="matmul",
    )(a, b)


def _rms_rows(a, g):
    return a * lax.rsqrt(jnp.mean(a * a, axis=-1, keepdims=True) + RMS_EPS) * g


def _rms_mm_kernel(a_ref, g_ref, b_ref, o_ref):
    an = _rms_rows(a_ref[...], g_ref[...]).astype(BF16)
    o_ref[...] = jnp.dot(an, b_ref[...], preferred_element_type=F32).astype(o_ref.dtype)


def _rms_matmul(a, g, b, out_dtype, bm, bn):
    m, k = a.shape
    n = b.shape[1]
    bm = min(bm, m)
    return pl.pallas_call(
        _rms_mm_kernel, grid=(n // bn, m // bm),
        in_specs=[pl.BlockSpec((bm, k), lambda j, i: (i, 0)),
                  pl.BlockSpec((1, k), lambda j, i: (0, 0)),
                  pl.BlockSpec((k, bn), lambda j, i: (0, j))],
        out_specs=pl.BlockSpec((bm, bn), lambda j, i: (i, j)),
        out_shape=jax.ShapeDtypeStruct((m, n), out_dtype),
        compiler_params=_params("parallel", "parallel"), name="rms_matmul",
    )(a, g, b)


def _gdn_gate_kernel(ab_ref, alog_ref, dtb_ref, beta_ref, gc_ref, *, chunk):
    b_raw = ab_ref[0:HEADS, :]
    a_raw = ab_ref[HEADS:2 * HEADS, :]
    beta_ref[...] = 1.0 / (1.0 + jnp.exp(-b_raw))
    xx = a_raw + dtb_ref[...]
    softplus = jnp.maximum(xx, 0.0) + jnp.log(1.0 + jnp.exp(-jnp.abs(xx)))
    g = -jnp.exp(alog_ref[...]) * softplus
    pos = lax.broadcasted_iota(jnp.int32, g.shape, 1) % chunk
    sh = 1
    while sh < chunk:
        g = g + jnp.where(pos >= sh, pltpu.roll(g, sh, axis=1), 0.0)
        sh *= 2
    gc_ref[...] = g


def _gdn_gates(ab_t, a_log, dt_bias, chunk, tn=2048):
    two_h, s = ab_t.shape
    tn = min(tn, s)
    col = pl.BlockSpec((HEADS, 1), lambda j: (0, 0))
    out = pl.BlockSpec((HEADS, tn), lambda j: (0, j))
    return pl.pallas_call(
        functools.partial(_gdn_gate_kernel, chunk=chunk), grid=(s // tn,),
        in_specs=[pl.BlockSpec((two_h, tn), lambda j: (0, j)), col, col],
        out_specs=[out, out],
        out_shape=[jax.ShapeDtypeStruct((HEADS, s), F32)] * 2,
        compiler_params=_params("parallel"), name="gdn_gates",
    )(ab_t, a_log.reshape(HEADS, 1), dt_bias.reshape(HEADS, 1))


def _bdot(a, b):
    return jnp.dot(a.astype(BF16), b.astype(BF16), preferred_element_type=F32)


def _gdn_core_kernel(q_ref, k_ref, v_ref, z_ref, gc_ref, beta_ref, wq_ref, wk_ref, wv_ref, ng_ref,
                     o_ref, state_ref, qbuf, kbuf, vbuf, *, chunk):
    bt = q_ref.shape[0]
    dk = q_ref.shape[1]
    halo = CONV_HALO

    @pl.when(pl.program_id(1) == 0)
    def _():
        state_ref[...] = jnp.zeros_like(state_ref)
        for buf in (qbuf, kbuf, vbuf):
            buf[0:halo, :] = jnp.zeros((halo, dk), F32)

    qbuf[halo:halo + bt, :] = q_ref[...].astype(F32)
    kbuf[halo:halo + bt, :] = k_ref[...].astype(F32)
    vbuf[halo:halo + bt, :] = v_ref[...].astype(F32)

    row = lax.broadcasted_iota(jnp.int32, (chunk, chunk), 0)
    col = lax.broadcasted_iota(jnp.int32, (chunk, chunk), 1)
    incl = row >= col
    strict = row > col
    eye = (row == col).astype(F32)

    def conv_silu(buf, w_ref, r0):
        ext = buf[pl.ds(r0, chunk + halo), :]
        w = w_ref[...]
        acc = ext[halo:, :] * w[GDN_CONV - 1:GDN_CONV, :]
        for j in range(GDN_CONV - 1):
            sh = GDN_CONV - 1 - j
            acc = acc + pltpu.roll(ext, sh, axis=0)[halo:, :] * w[j:j + 1, :]
        return _silu(acc)

    def l2n(x):
        return x * lax.rsqrt(jnp.sum(x * x, axis=-1, keepdims=True) + RMS_EPS)

    def body(c, state):
        r0 = pl.multiple_of(c * chunk, chunk)
        q = l2n(conv_silu(qbuf, wq_ref, r0)) * (dk ** -0.5)
        k = l2n(conv_silu(kbuf, wk_ref, r0))
        v = conv_silu(vbuf, wv_ref, r0)

        gc_rb = jnp.broadcast_to(gc_ref[:, pl.ds(r0, chunk)], (chunk, chunk))
        bt_rb = jnp.broadcast_to(beta_ref[:, pl.ds(r0, chunk)], (chunk, chunk))
        gc_cb = gc_rb.T
        bt_cb = bt_rb.T
        decay = jnp.where(incl, jnp.exp(jnp.where(incl, gc_cb - gc_rb, 0.0)), 0.0)

        k16 = k.astype(BF16)
        qkk = lax.dot_general(jnp.concatenate([q, k], axis=0).astype(BF16), k16,
                              (((1,), (1,)), ((), ())), preferred_element_type=F32)
        qk = qkk[:chunk] * decay
        a_mat = jnp.where(strict, qkk[chunk:] * bt_cb * decay, 0.0)

        p_mat = eye - a_mat
        m_mat = _bdot(a_mat, a_mat)
        n_sq = 1
        while 2 * n_sq < chunk // 2:
            pm = _bdot(jnp.concatenate([p_mat, m_mat], axis=0), m_mat)
            p_mat = p_mat + pm[:chunk]
            m_mat = pm[chunk:]
            n_sq += n_sq
        p_mat = p_mat + _bdot(p_mat, m_mat)

        e_gc = jnp.exp(gc_cb)
        uw = _bdot(p_mat, jnp.concatenate([v * bt_cb, k * (bt_cb * e_gc)], axis=1))
        u = uw[:, :dk]
        w = uw[:, dk:]

        ws = _bdot(jnp.concatenate([w, q * e_gc], axis=0), state)
        v_new = u - ws[:chunk]
        o = ws[chunk:] + _bdot(qk, v_new)
        gl_row = gc_cb[chunk - 1:chunk, :]
        k_dec = k * jnp.exp(gl_row - gc_cb)
        state = state * jnp.exp(gl_row) + _bdot(k_dec.T, v_new)

        on = o * lax.rsqrt(jnp.mean(o * o, axis=-1, keepdims=True) + RMS_EPS) * ng_ref[...]
        z = z_ref[pl.ds(r0, chunk), :].astype(F32)
        o_ref[pl.ds(r0, chunk), :] = (on * _silu(z)).astype(o_ref.dtype)
        return state

    state_ref[...] = lax.fori_loop(0, bt // chunk, body, state_ref[...])
    for buf in (qbuf, kbuf, vbuf):
        buf[0:halo, :] = buf[bt:bt + halo, :]


def _gdn_core(proj, gc, beta, w_conv, norm_g, chunk=GDN_CHUNK, bt=GDN_BLOCK):
    s = proj.shape[0]
    bt = min(bt, s)
    dk = HEAD_DIM
    tok = lambda off: pl.BlockSpec((bt, dk), lambda h, b: (b, off + h))
    gate = pl.BlockSpec((None, 1, bt), lambda h, b: (h, 0, b))
    cw = lambda off: pl.BlockSpec((GDN_CONV, dk), lambda h, b: (0, off + h))
    return pl.pallas_call(
        functools.partial(_gdn_core_kernel, chunk=chunk),
        grid=(HEADS, s // bt),
        in_specs=[tok(0), tok(HEADS), tok(2 * HEADS), tok(3 * HEADS), gate, gate,
                  cw(0), cw(HEADS), cw(2 * HEADS), pl.BlockSpec((1, dk), lambda h, b: (0, 0))],
        out_specs=pl.BlockSpec((bt, dk), lambda h, b: (b, h)),
        out_shape=jax.ShapeDtypeStruct((s, HEADS * dk), BF16),
        scratch_shapes=[pltpu.VMEM((dk, dk), F32)] + [pltpu.VMEM((bt + 2 * CONV_HALO, dk), F32)] * 3,
        compiler_params=_params("parallel", "arbitrary"), name="gdn_core",
    )(proj, proj, proj, proj, gc.reshape(HEADS, 1, s), beta.reshape(HEADS, 1, s),
      w_conv, w_conv, w_conv, norm_g.reshape(1, dk))


def _rope_tab_kernel(pos_ref, invf_ref, cm_ref, sn_ref, sp_ref):
    ang = pos_ref[...].astype(F32) * invf_ref[...]
    cos = jnp.cos(ang)
    sin = jnp.sin(ang)
    lane = lax.broadcasted_iota(jnp.int32, ang.shape, 1)
    half = MLA_ROPE // 2
    rope = (lane >= HEAD_DIM) & (lane < HEAD_DIM + MLA_ROPE)
    cm_ref[...] = jnp.where(lane < HEAD_DIM, 1.0, jnp.where(rope, cos, 0.0))
    sn_ref[...] = jnp.where(rope & (lane < HEAD_DIM + half), -sin, 0.0)
    sp_ref[...] = jnp.where(rope & (lane >= HEAD_DIM + half), sin, 0.0)


def _rope_tables(positions, bm=512):
    s = positions.shape[0]
    bm = min(bm, s)
    half = MLA_ROPE // 2
    inv_freq = ROPE_THETA ** (-jnp.arange(0, half, dtype=F32) / half)
    invf = jnp.concatenate([jnp.zeros((HEAD_DIM,), F32), inv_freq, inv_freq,
                            jnp.zeros((MLA_QK_PAD - MLA_QK,), F32)]).reshape(1, MLA_QK_PAD)
    tab = pl.BlockSpec((bm, MLA_QK_PAD), lambda i: (i, 0))
    return pl.pallas_call(
        _rope_tab_kernel, grid=(s // bm,),
        in_specs=[pl.BlockSpec((bm, 1), lambda i: (i, 0)), pl.BlockSpec((1, MLA_QK_PAD), lambda i: (0, 0))],
        out_specs=[tab, tab, tab],
        out_shape=[jax.ShapeDtypeStruct((s, MLA_QK_PAD), F32)] * 3,
        compiler_params=_params("parallel"), name="rope_tables",
    )(positions.reshape(s, 1), invf)


def _rope(x, cm, sn, sp):
    w = MLA_QK_PAD
    half = MLA_ROPE // 2
    return x * cm + pltpu.roll(x, w - half, axis=1) * sn + pltpu.roll(x, half, axis=1) * sp


def _q_proj_kernel(a_ref, g_ref, w_ref, cm_ref, sn_ref, sp_ref, o_ref):
    an = _rms_rows(a_ref[...], g_ref[...]).astype(BF16)
    acc = jnp.dot(an, w_ref[...], preferred_element_type=F32)
    cm, sn, sp = cm_ref[...], sn_ref[...], sp_ref[...]
    scale = MLA_QK ** -0.5
    for g in range(o_ref.shape[1] // MLA_QK_PAD):
        sl = slice(g * MLA_QK_PAD, (g + 1) * MLA_QK_PAD)
        o_ref[:, sl] = (_rope(acc[:, sl], cm, sn, sp) * scale).astype(o_ref.dtype)


def _k_proj_kernel(a_ref, g_ref, w_ref, kr_ref, cm_ref, sn_ref, sp_ref, o_ref):
    an = _rms_rows(a_ref[...], g_ref[...]).astype(BF16)
    acc = jnp.dot(an, w_ref[...], preferred_element_type=F32)
    kr = _rope(kr_ref[...], cm_ref[...], sn_ref[...], sp_ref[...])
    for g in range(o_ref.shape[1] // MLA_QK_PAD):
        sl = slice(g * MLA_QK_PAD, (g + 1) * MLA_QK_PAD)
        o_ref[:, sl] = (acc[:, sl] + kr).astype(o_ref.dtype)


def _qk_proj(kernel, a, g, w, extras, bm=512, bn=2048):
    m, k = a.shape
    n = w.shape[1]
    bm = min(bm, m)
    tab = pl.BlockSpec((bm, MLA_QK_PAD), lambda j, i: (i, 0))
    return pl.pallas_call(
        kernel, grid=(n // bn, m // bm),
        in_specs=[pl.BlockSpec((bm, k), lambda j, i: (i, 0)),
                  pl.BlockSpec((1, k), lambda j, i: (0, 0)),
                  pl.BlockSpec((k, bn), lambda j, i: (0, j))] + [tab] * len(extras),
        out_specs=pl.BlockSpec((bm, bn), lambda j, i: (i, j)),
        out_shape=jax.ShapeDtypeStruct((m, n), BF16),
        compiler_params=_params("parallel", "parallel"), name=kernel.__name__.strip("_"),
    )(a, g, w, *extras)


def _attn_kernel(qi_ref, kj_ref, q_ref, k_ref, v_ref, z_ref, o_ref, m_ref, l_ref, acc_ref):
    p_idx = pl.program_id(1)
    i = qi_ref[p_idx]
    j = kj_ref[p_idx]
    tq, tk = q_ref.shape[0], k_ref.shape[0]

    @pl.when(j == 0)
    def _():
        m_ref[...] = jnp.full_like(m_ref, -jnp.inf)
        l_ref[...] = jnp.zeros_like(l_ref)
        acc_ref[...] = jnp.zeros_like(acc_ref)

    def step(masked):
        s = lax.dot_general(q_ref[...], k_ref[...], (((1,), (1,)), ((), ())), preferred_element_type=F32)
        if masked:
            row = lax.broadcasted_iota(jnp.int32, (tq, tk), 0)
            col = lax.broadcasted_iota(jnp.int32, (tq, tk), 1)
            s = jnp.where(col <= row, s, -jnp.inf)
        m_prev = m_ref[...]
        m_new = jnp.maximum(m_prev, jnp.max(s, axis=-1, keepdims=True))
        alpha = jnp.exp(m_prev - m_new)
        p = jnp.exp(s - m_new)
        l_ref[...] = alpha * l_ref[...] + jnp.sum(p, axis=-1, keepdims=True)
        acc_ref[...] = alpha * acc_ref[...] + jnp.dot(p.astype(BF16), v_ref[...], preferred_element_type=F32)
        m_ref[...] = m_new

    @pl.when(j < i)
    def _():
        step(False)

    @pl.when(j == i)
    def _():
        step(True)
        o = acc_ref[...] / l_ref[...]
        o_ref[...] = (o * _silu(z_ref[...].astype(F32))).astype(o_ref.dtype)


def _attention(q, k, v, z, tq=ATT_TQ):
    s = q.shape[0]
    tq = min(tq, s)
    nq = s // tq
    pairs = [(i, j) for i in range(nq) for j in range(i + 1)]
    qi = jnp.asarray([p[0] for p in pairs], jnp.int32)
    kj = jnp.asarray([p[1] for p in pairs], jnp.int32)
    grid_spec = pltpu.PrefetchScalarGridSpec(
        num_scalar_prefetch=2, grid=(HEADS, len(pairs)),
        in_specs=[pl.BlockSpec((tq, MLA_QK_PAD), lambda h, p, qi, kj: (qi[p], h)),
                  pl.BlockSpec((tq, MLA_QK_PAD), lambda h, p, qi, kj: (kj[p], h)),
                  pl.BlockSpec((tq, HEAD_DIM), lambda h, p, qi, kj: (kj[p], h)),
                  pl.BlockSpec((tq, HEAD_DIM), lambda h, p, qi, kj: (qi[p], h))],
        out_specs=pl.BlockSpec((tq, HEAD_DIM), lambda h, p, qi, kj: (qi[p], h)),
        scratch_shapes=[pltpu.VMEM((tq, 1), F32), pltpu.VMEM((tq, 1), F32), pltpu.VMEM((tq, HEAD_DIM), F32)])
    return pl.pallas_call(
        _attn_kernel, grid_spec=grid_spec,
        out_shape=jax.ShapeDtypeStruct((s, HEADS * HEAD_DIM), BF16),
        compiler_params=_params("parallel", "arbitrary"), name="mla_attention",
    )(qi, kj, q, k, v, z)


def _gdn_branch(h, w_in, w_conv, a_log, dt_bias, norm_g, w_out):
    width = 4 * HEADS * HEAD_DIM
    proj = _matmul(h, w_in[:, :width].astype(BF16), BF16, bm=1024, bn=1024)
    w_gate = jnp.pad(w_in[:, width:], ((0, 0), (0, 128 - 2 * HEADS))).astype(BF16)
    ab = _matmul(h, w_gate, F32, bm=1024, bn=128)[:, :2 * HEADS]
    beta, gc = _gdn_gates(ab.T, a_log, dt_bias, GDN_CHUNK)
    gated = _gdn_core(proj, gc, beta, w_conv, norm_g)
    return _matmul(gated, w_out.astype(BF16), F32, bm=1024, bn=1024)


def _mla_branch(h, positions, w_in, q_norm_g, w_qb, kv_norm_g, w_kvb, w_out):
    s = h.shape[0]
    n_lat = MLA_Q_RANK + MLA_KV_RANK + MLA_ROPE
    lat = _matmul(h, w_in[:, :n_lat].astype(BF16), F32, bm=512, bn=n_lat)
    z = _matmul(h, w_in[:, n_lat:].astype(BF16), BF16, bm=1024, bn=1024)
    cq = lat[:, :MLA_Q_RANK]
    ckv = lat[:, MLA_Q_RANK:MLA_Q_RANK + MLA_KV_RANK]
    k_rope = jnp.pad(lat[:, MLA_Q_RANK + MLA_KV_RANK:], ((0, 0), (HEAD_DIM, MLA_QK_PAD - MLA_QK)))

    wq = w_qb.reshape(MLA_Q_RANK, HEADS, MLA_QK)
    wq = jnp.pad(wq, ((0, 0), (0, 0), (0, MLA_QK_PAD - MLA_QK))).reshape(MLA_Q_RANK, HEADS * MLA_QK_PAD)
    wkv = w_kvb.reshape(MLA_KV_RANK, HEADS, 2 * HEAD_DIM)
    wk = jnp.pad(wkv[:, :, :HEAD_DIM], ((0, 0), (0, 0), (0, MLA_QK_PAD - HEAD_DIM)))
    wk = wk.reshape(MLA_KV_RANK, HEADS * MLA_QK_PAD)
    wv = wkv[:, :, HEAD_DIM:].reshape(MLA_KV_RANK, HEADS * HEAD_DIM)

    cm, sn, sp = _rope_tables(positions.reshape(s))
    q = _qk_proj(_q_proj_kernel, cq, q_norm_g.reshape(1, -1), wq.astype(BF16), (cm, sn, sp))
    k = _qk_proj(_k_proj_kernel, ckv, kv_norm_g.reshape(1, -1), wk.astype(BF16), (k_rope, cm, sn, sp))
    v = _rms_matmul(ckv, kv_norm_g.reshape(1, -1), wv.astype(BF16), BF16, bm=1024, bn=1024)
    gated = _attention(q, k, v, z)
    return _matmul(gated, w_out.astype(BF16), F32, bm=1024, bn=1024)


def kernel(x, c, positions, w_mod, b_mod, ln_g, ln_b, a_w_in, a_w_conv, a_a_log, a_dt_bias, a_norm_g, a_w_out,
           b_w_in, b_q_norm_g, b_w_qb, b_kv_norm_g, b_w_kvb, b_w_out):
    bsz, s, d = x.shape
    assert bsz == 1 and d == D_MODEL and DEPTH == 2
    x0 = x.reshape(s, d)
    mod = _modulation(c, w_mod, b_mod)
    vec = lambda t: t.reshape(1, d)

    h0 = _modulate(x0, mod[0, 1], mod[0, 0])
    y0 = _gdn_branch(h0, a_w_in[0], a_w_conv[0], a_a_log[0], a_dt_bias[0], a_norm_g[0], a_w_out[0])
    x1, h1 = _residual_layernorm(x0, y0, mod[0, 2], vec(ln_g[0]), vec(ln_b[0]), mod[1, 1], mod[1, 0])
    y1 = _mla_branch(h1, positions, b_w_in[0], b_q_norm_g[0], b_w_qb[0], b_kv_norm_g[0], b_w_kvb[0], b_w_out[0])
    x2 = _residual_layernorm(x1, y1, mod[1, 2], vec(ln_g[1]), vec(ln_b[1]))
    return x2.reshape(bsz, s, d)
```

```python
import functools

import jax
import jax.numpy as jnp
from jax import lax
from jax.experimental import pallas as pl
from jax.experimental.pallas import tpu as pltpu

F32 = jnp.float32
BF16 = jnp.bfloat16

D_MODEL = 4096
DEPTH = 2
DEEPNORM_ALPHA = (2.0 * DEPTH) ** 0.25
RMS_EPS = 1e-6
LN_EPS = 1e-5

HEADS = 32
HEAD_DIM = 128
GDN_CONV = 4
GDN_CHUNK = 128
GDN_BLOCK = 1024
GDN_HEADS_PER_STEP = 2
GDN_INV_BASE = 32
CONV_HALO = 8

MLA_Q_RANK = 896
MLA_KV_RANK = 512
MLA_ROPE = 64
MLA_QK = HEAD_DIM + MLA_ROPE
MLA_QK_PAD = 256
ROPE_THETA = 10000.0
LOG2_E = 1.4426950408889634
ATT_TQ = 512
ATT_TK = 512
ATT_GROUPS = 2

VMEM_LIMIT = 56 * 1024 * 1024


def _params(*sem):
    return pltpu.CompilerParams(dimension_semantics=sem, vmem_limit_bytes=VMEM_LIMIT)


def _silu(x):
    return x / (1.0 + jnp.exp(-x))


def _mod_kernel(c_ref, w_ref, b_ref, o_ref):
    tn = o_ref.shape[-1]
    rows = 128

    def body(kk, acc):
        r = pl.multiple_of(kk * rows, rows)
        c = c_ref[pl.ds(r, rows), :]
        a = _silu(c)
        p = w_ref[pl.ds(r, rows), :] * jnp.concatenate([a] * (tn // 128), axis=1)
        for t in range(rows // 8):
            acc = acc + p[t * 8:(t + 1) * 8, :]
        return acc

    acc = lax.fori_loop(0, c_ref.shape[0] // rows, body, jnp.zeros((8, tn), F32))
    o_ref[...] = jnp.sum(acc, axis=0, keepdims=True) + b_ref[...]


def _modulation(c, w_mod, b_mod, tn=512):
    depth, d, n = w_mod.shape
    c_rep = jnp.broadcast_to(c.reshape(d, 1), (d, 128))
    out = pl.pallas_call(
        _mod_kernel,
        grid=(depth, n // tn),
        in_specs=[pl.BlockSpec((d, 128), lambda l, j: (0, 0)),
                  pl.BlockSpec((None, d, tn), lambda l, j: (l, 0, j)),
                  pl.BlockSpec((None, 1, tn), lambda l, j: (l, 0, j))],
        out_specs=pl.BlockSpec((None, 1, tn), lambda l, j: (l, 0, j)),
        out_shape=jax.ShapeDtypeStruct((depth, 1, n), F32),
        compiler_params=_params("parallel", "parallel"),
        name="adaln_mod",
    )(c_rep, w_mod, b_mod.reshape(depth, 1, n))
    return out.reshape(depth, 3, 1, d)


def _modulate_kernel(x_ref, sc_ref, sh_ref, h_ref):
    h_ref[...] = (x_ref[...] * (1.0 + sc_ref[...]) + sh_ref[...]).astype(h_ref.dtype)


def _modulate(x, scale, shift, bm=512):
    s, d = x.shape
    row = pl.BlockSpec((bm, d), lambda i: (i, 0))
    vec = pl.BlockSpec((1, d), lambda i: (0, 0))
    return pl.pallas_call(
        _modulate_kernel, grid=(s // bm,), in_specs=[row, vec, vec], out_specs=row,
        out_shape=jax.ShapeDtypeStruct((s, d), BF16),
        compiler_params=_params("parallel"), name="modulate",
    )(x, scale, shift)


def _resid_ln(x_ref, y_ref, gate_ref, g_ref, b_ref):
    t = DEEPNORM_ALPHA * x_ref[...] + (1.0 + gate_ref[...]) * y_ref[...].astype(F32)
    mu = jnp.mean(t, axis=-1, keepdims=True)
    dlt = t - mu
    var = jnp.mean(dlt * dlt, axis=-1, keepdims=True)
    return dlt * lax.rsqrt(var + LN_EPS) * g_ref[...] + b_ref[...]


def _resid_ln_kernel(x_ref, y_ref, gate_ref, g_ref, b_ref, xo_ref):
    xo_ref[...] = _resid_ln(x_ref, y_ref, gate_ref, g_ref, b_ref)


def _resid_ln_mod_kernel(x_ref, y_ref, gate_ref, g_ref, b_ref, sc_ref, sh_ref, xo_ref, h_ref):
    xn = _resid_ln(x_ref, y_ref, gate_ref, g_ref, b_ref)
    xo_ref[...] = xn
    h_ref[...] = (xn * (1.0 + sc_ref[...]) + sh_ref[...]).astype(h_ref.dtype)


def _residual_layernorm(x, y, gate, ln_g, ln_b, next_scale=None, next_shift=None, bm=256):
    s, d = x.shape
    row = pl.BlockSpec((bm, d), lambda i: (i, 0))
    vec = pl.BlockSpec((1, d), lambda i: (0, 0))
    if next_scale is None:
        return pl.pallas_call(
            _resid_ln_kernel, grid=(s // bm,), in_specs=[row, row, vec, vec, vec], out_specs=row,
            out_shape=jax.ShapeDtypeStruct((s, d), F32),
            compiler_params=_params("parallel"), name="resid_ln",
        )(x, y, gate, ln_g, ln_b)
    return pl.pallas_call(
        _resid_ln_mod_kernel, grid=(s // bm,), in_specs=[row, row, vec, vec, vec, vec, vec],
        out_specs=[row, row],
        out_shape=[jax.ShapeDtypeStruct((s, d), F32), jax.ShapeDtypeStruct((s, d), BF16)],
        compiler_params=_params("parallel"), name="resid_ln_mod",
    )(x, y, gate, ln_g, ln_b, next_scale, next_shift)


def _mm_kernel(a_ref, b_ref, o_ref):
    o_ref[...] = jnp.dot(a_ref[...], b_ref[...], preferred_element_type=F32).astype(o_ref.dtype)


def _matmul(a, b, out_dtype, bm, bn):
    m, k = a.shape
    n = b.shape[1]
    bm = min(bm, m)
    return pl.pallas_call(
        _mm_kernel, grid=(n // bn, m // bm),
        in_specs=[pl.BlockSpec((bm, k), lambda j, i: (i, 0)),
                  pl.BlockSpec((k, bn), lambda j, i: (0, j))],
        out_specs=pl.BlockSpec((bm, bn), lambda j, i: (i, j)),
        out_shape=jax.ShapeDtypeStruct((m, n), out_dtype),
        compiler_params=_params("parallel", "parallel"), name="matmul",
    )(a, b)


def _rms_rows(a, g):
    return a * lax.rsqrt(jnp.mean(a * a, axis=-1, keepdims=True) + RMS_EPS) * g


def _rms_mm_kernel(a_ref, g_ref, b_ref, o_ref):
    an = _rms_rows(a_ref[...], g_ref[...]).astype(BF16)
    o_ref[...] = jnp.dot(an, b_ref[...], preferred_element_type=F32).astype(o_ref.dtype)


def _rms_matmul(a, g, b, out_dtype, bm, bn):
    m, k = a.shape
    n = b.shape[1]
    bm = min(bm, m)
    return pl.pallas_call(
        _rms_mm_kernel, grid=(n // bn, m // bm),
        in_specs=[pl.BlockSpec((bm, k), lambda j, i: (i, 0)),
                  pl.BlockSpec((1, k), lambda j, i: (0, 0)),
                  pl.BlockSpec((k, bn), lambda j, i: (0, j))],
        out_specs=pl.BlockSpec((bm, bn), lambda j, i: (i, j)),
        out_shape=jax.ShapeDtypeStruct((m, n), out_dtype),
        compiler_params=_params("parallel", "parallel"), name="rms_matmul",
    )(a, g, b)


def _gdn_gate_kernel(ab_ref, alog_ref, dtb_ref, beta_ref, gc_ref, *, chunk):
    b_raw = ab_ref[0:HEADS, :]
    a_raw = ab_ref[HEADS:2 * HEADS, :]
    beta_ref[...] = 1.0 / (1.0 + jnp.exp(-b_raw))
    xx = a_raw + dtb_ref[...]
    softplus = jnp.maximum(xx, 0.0) + jnp.log(1.0 + jnp.exp(-jnp.abs(xx)))
    g = -jnp.exp(alog_ref[...]) * softplus
    pos = lax.broadcasted_iota(jnp.int32, g.shape, 1) % chunk
    sh = 1
    while sh < chunk:
        g = g + jnp.where(pos >= sh, pltpu.roll(g, sh, axis=1), 0.0)
        sh *= 2
    gc_ref[...] = g


def _gdn_gates(ab_t, a_log, dt_bias, chunk, tn=2048):
    two_h, s = ab_t.shape
    tn = min(tn, s)
    col = pl.BlockSpec((HEADS, 1), lambda j: (0, 0))
    out = pl.BlockSpec((HEADS, tn), lambda j: (0, j))
    return pl.pallas_call(
        functools.partial(_gdn_gate_kernel, chunk=chunk), grid=(s // tn,),
        in_specs=[pl.BlockSpec((two_h, tn), lambda j: (0, j)), col, col],
        out_specs=[out, out],
        out_shape=[jax.ShapeDtypeStruct((HEADS, s), F32)] * 2,
        compiler_params=_params("parallel"), name="gdn_gates",
    )(ab_t, a_log.reshape(HEADS, 1), dt_bias.reshape(HEADS, 1))


def _bdot(a, b):
    return jnp.dot(a.astype(BF16), b.astype(BF16), preferred_element_type=F32)


def _unit_lower_inverse(a_mat, row, col, eye, size, base):
    same = lambda b: (row // b) == (col // b)
    a0 = jnp.where(same(base), a_mat, 0.0)
    p = eye - a0
    m = _bdot(a0, a0)
    n = 1
    while 2 * n < base // 2:
        pm = _bdot(jnp.concatenate([p, m], axis=0), m)
        p = p + pm[:size]
        m = pm[size:]
        n += n
    p = p + _bdot(p, m)
    b = base
    while b < size:
        a_off = jnp.where(same(2 * b) & jnp.logical_not(same(b)), a_mat, 0.0)
        p = p - _bdot(_bdot(p, a_off), p)
        b *= 2
    return p


def _gdn_core_kernel(q_ref, k_ref, v_ref, z_ref, gc_ref, beta_ref, wq_ref, wk_ref, wv_ref, ng_ref,
                     o_ref, state_ref, qbuf, kbuf, vbuf, *, chunk):
    bt = q_ref.shape[0]
    dk = HEAD_DIM
    nh = q_ref.shape[1] // dk
    halo = CONV_HALO

    @pl.when(pl.program_id(1) == 0)
    def _():
        state_ref[...] = jnp.zeros_like(state_ref)
        for buf in (qbuf, kbuf, vbuf):
            buf[0:halo, :] = jnp.zeros((halo, nh * dk), F32)

    qbuf[halo:halo + bt, :] = q_ref[...].astype(F32)
    kbuf[halo:halo + bt, :] = k_ref[...].astype(F32)
    vbuf[halo:halo + bt, :] = v_ref[...].astype(F32)

    row = lax.broadcasted_iota(jnp.int32, (chunk, chunk), 0)
    col = lax.broadcasted_iota(jnp.int32, (chunk, chunk), 1)
    incl = row >= col
    strict = row > col
    eye = (row == col).astype(F32)

    def conv_silu(buf, w_ref, r0, hh):
        lanes = slice(hh * dk, (hh + 1) * dk)
        ext = buf[pl.ds(r0, chunk + halo), lanes]
        w = w_ref[:, lanes]
        acc = ext[halo:, :] * w[GDN_CONV - 1:GDN_CONV, :]
        for j in range(GDN_CONV - 1):
            sh = GDN_CONV - 1 - j
            acc = acc + pltpu.roll(ext, sh, axis=0)[halo:, :] * w[j:j + 1, :]
        return _silu(acc)

    def l2n(x):
        return x * lax.rsqrt(jnp.sum(x * x, axis=-1, keepdims=True) + RMS_EPS)

    def head_chunk(hh, r0, state):
        lanes = slice(hh * dk, (hh + 1) * dk)
        q = l2n(conv_silu(qbuf, wq_ref, r0, hh)) * (dk ** -0.5)
        k = l2n(conv_silu(kbuf, wk_ref, r0, hh))
        v = conv_silu(vbuf, wv_ref, r0, hh)

        gc_rb = jnp.broadcast_to(gc_ref[hh:hh + 1, pl.ds(r0, chunk)], (chunk, chunk))
        bt_rb = jnp.broadcast_to(beta_ref[hh:hh + 1, pl.ds(r0, chunk)], (chunk, chunk))
        gc_cb = gc_rb.T
        bt_cb = bt_rb.T
        decay = jnp.where(incl, jnp.exp(jnp.where(incl, gc_cb - gc_rb, 0.0)), 0.0)

        qkk = lax.dot_general(jnp.concatenate([q, k], axis=0).astype(BF16), k.astype(BF16),
                              (((1,), (1,)), ((), ())), preferred_element_type=F32)
        qk = qkk[:chunk] * decay
        a_mat = jnp.where(strict, qkk[chunk:] * bt_cb * decay, 0.0)
        t_mat = _unit_lower_inverse(a_mat, row, col, eye, chunk, GDN_INV_BASE)

        e_gc = jnp.exp(gc_cb)
        uw = _bdot(t_mat, jnp.concatenate([v * bt_cb, k * (bt_cb * e_gc)], axis=1))
        u = uw[:, :dk]
        w = uw[:, dk:]

        ws = _bdot(jnp.concatenate([w, q * e_gc], axis=0), state)
        v_new = u - ws[:chunk]
        o = ws[chunk:] + _bdot(qk, v_new)
        gl_row = gc_cb[chunk - 1:chunk, :]
        k_dec = k * jnp.exp(gl_row - gc_cb)
        state = state * jnp.exp(gl_row) + _bdot(k_dec.T, v_new)

        on = o * lax.rsqrt(jnp.mean(o * o, axis=-1, keepdims=True) + RMS_EPS) * ng_ref[...]
        z = z_ref[pl.ds(r0, chunk), lanes].astype(F32)
        o_ref[pl.ds(r0, chunk), lanes] = (on * _silu(z)).astype(o_ref.dtype)
        return state

    def body(c, states):
        r0 = pl.multiple_of(c * chunk, chunk)
        return tuple(head_chunk(hh, r0, states[hh]) for hh in range(nh))

    states = lax.fori_loop(0, bt // chunk, body, tuple(state_ref[hh] for hh in range(nh)))
    for hh in range(nh):
        state_ref[hh] = states[hh]
    for buf in (qbuf, kbuf, vbuf):
        buf[0:halo, :] = buf[bt:bt + halo, :]


def _gdn_core(proj, gc, beta, w_conv, norm_g, chunk=GDN_CHUNK, bt=GDN_BLOCK, nh=GDN_HEADS_PER_STEP):
    s = proj.shape[0]
    bt = min(bt, s)
    dk = HEAD_DIM
    groups = HEADS // nh
    tok = lambda part: pl.BlockSpec((bt, nh * dk), lambda h, b: (b, part * groups + h))
    gate = pl.BlockSpec((None, nh, bt), lambda h, b: (h, 0, b))
    cw = lambda part: pl.BlockSpec((GDN_CONV, nh * dk), lambda h, b: (0, part * groups + h))
    return pl.pallas_call(
        functools.partial(_gdn_core_kernel, chunk=chunk),
        grid=(groups, s // bt),
        in_specs=[tok(0), tok(1), tok(2), tok(3), gate, gate,
                  cw(0), cw(1), cw(2), pl.BlockSpec((1, dk), lambda h, b: (0, 0))],
        out_specs=pl.BlockSpec((bt, nh * dk), lambda h, b: (b, h)),
        out_shape=jax.ShapeDtypeStruct((s, HEADS * dk), BF16),
        scratch_shapes=[pltpu.VMEM((nh, dk, dk), F32)] + [pltpu.VMEM((bt + 2 * CONV_HALO, nh * dk), F32)] * 3,
        compiler_params=_params("parallel", "arbitrary"), name="gdn_core",
    )(proj, proj, proj, proj, gc.reshape(groups, nh, s), beta.reshape(groups, nh, s),
      w_conv, w_conv, w_conv, norm_g.reshape(1, dk))


def _rope_tab_kernel(pos_ref, invf_ref, cm_ref, sn_ref, sp_ref):
    ang = pos_ref[...].astype(F32) * invf_ref[...]
    cos = jnp.cos(ang)
    sin = jnp.sin(ang)
    lane = lax.broadcasted_iota(jnp.int32, ang.shape, 1)
    half = MLA_ROPE // 2
    rope = (lane >= HEAD_DIM) & (lane < HEAD_DIM + MLA_ROPE)
    cm_ref[...] = jnp.where(lane < HEAD_DIM, 1.0, jnp.where(rope, cos, 0.0))
    sn_ref[...] = jnp.where(rope & (lane < HEAD_DIM + half), -sin, 0.0)
    sp_ref[...] = jnp.where(rope & (lane >= HEAD_DIM + half), sin, 0.0)


def _rope_tables(positions, bm=512):
    s = positions.shape[0]
    bm = min(bm, s)
    half = MLA_ROPE // 2
    inv_freq = ROPE_THETA ** (-jnp.arange(0, half, dtype=F32) / half)
    invf = jnp.concatenate([jnp.zeros((HEAD_DIM,), F32), inv_freq, inv_freq,
                            jnp.zeros((MLA_QK_PAD - MLA_QK,), F32)]).reshape(1, MLA_QK_PAD)
    tab = pl.BlockSpec((bm, MLA_QK_PAD), lambda i: (i, 0))
    return pl.pallas_call(
        _rope_tab_kernel, grid=(s // bm,),
        in_specs=[pl.BlockSpec((bm, 1), lambda i: (i, 0)), pl.BlockSpec((1, MLA_QK_PAD), lambda i: (0, 0))],
        out_specs=[tab, tab, tab],
        out_shape=[jax.ShapeDtypeStruct((s, MLA_QK_PAD), F32)] * 3,
        compiler_params=_params("parallel"), name="rope_tables",
    )(positions.reshape(s, 1), invf)


def _rope(x, cm, sn, sp):
    w = MLA_QK_PAD
    half = MLA_ROPE // 2
    return x * cm + pltpu.roll(x, w - half, axis=1) * sn + pltpu.roll(x, half, axis=1) * sp


def _q_proj_kernel(a_ref, g_ref, w_ref, cm_ref, sn_ref, sp_ref, o_ref):
    an = _rms_rows(a_ref[...], g_ref[...]).astype(BF16)
    acc = jnp.dot(an, w_ref[...], preferred_element_type=F32)
    cm, sn, sp = cm_ref[...], sn_ref[...], sp_ref[...]
    scale = MLA_QK ** -0.5 * LOG2_E
    for g in range(o_ref.shape[1] // MLA_QK_PAD):
        sl = slice(g * MLA_QK_PAD, (g + 1) * MLA_QK_PAD)
        o_ref[:, sl] = (_rope(acc[:, sl], cm, sn, sp) * scale).astype(o_ref.dtype)


def _k_proj_kernel(a_ref, g_ref, w_ref, kr_ref, cm_ref, sn_ref, sp_ref, o_ref):
    an = _rms_rows(a_ref[...], g_ref[...]).astype(BF16)
    acc = jnp.dot(an, w_ref[...], preferred_element_type=F32)
    kr = _rope(kr_ref[...], cm_ref[...], sn_ref[...], sp_ref[...])
    for g in range(o_ref.shape[1] // MLA_QK_PAD):
        sl = slice(g * MLA_QK_PAD, (g + 1) * MLA_QK_PAD)
        o_ref[:, sl] = (acc[:, sl] + kr).astype(o_ref.dtype)


def _qk_proj(kernel, a, g, w, extras, bm=512, bn=2048):
    m, k = a.shape
    n = w.shape[1]
    bm = min(bm, m)
    tab = pl.BlockSpec((bm, MLA_QK_PAD), lambda j, i: (i, 0))
    return pl.pallas_call(
        kernel, grid=(n // bn, m // bm),
        in_specs=[pl.BlockSpec((bm, k), lambda j, i: (i, 0)),
                  pl.BlockSpec((1, k), lambda j, i: (0, 0)),
                  pl.BlockSpec((k, bn), lambda j, i: (0, j))] + [tab] * len(extras),
        out_specs=pl.BlockSpec((bm, bn), lambda j, i: (i, j)),
        out_shape=jax.ShapeDtypeStruct((m, n), BF16),
        compiler_params=_params("parallel", "parallel"), name=kernel.__name__.strip("_"),
    )(a, g, w, *extras)


def _attn_kernel(q_ref, k_ref, v_ref, z_ref, o_ref, sa_ref, sb_ref, m_ref, acc_ref, *, tk):
    i = pl.program_id(1)
    tq = q_ref.shape[0]
    dv = v_ref.shape[1]
    rows = tq // ATT_GROUPS
    lane_tiles = tk // dv

    def scores(s_ref, blk):
        k_blk = k_ref[pl.ds(pl.multiple_of(blk * tk, tk), tk), :]
        for g in range(ATT_GROUPS):
            q = q_ref[g * rows:(g + 1) * rows, :]
            s_ref[g] = lax.dot_general(q, k_blk, (((1,), (1,)), ((), ())), preferred_element_type=F32)

    def softmax_pv(s_ref, blk, masked):
        v_blk = v_ref[pl.ds(pl.multiple_of(blk * tk, tk), tk), :]
        v_ext = jnp.concatenate([v_blk, jnp.ones((tk, dv), BF16)], axis=1)
        for g in range(ATT_GROUPS):
            s = s_ref[g]
            if masked:
                row = lax.broadcasted_iota(jnp.int32, (rows, tk), 0) + g * rows
                col = lax.broadcasted_iota(jnp.int32, (rows, tk), 1)
                s = jnp.where(col <= row, s, -jnp.inf)
            m_part = s[:, :dv]
            for t in range(1, lane_tiles):
                m_part = jnp.maximum(m_part, s[:, t * dv:(t + 1) * dv])
            m_prev = m_ref[g]
            m_new = jnp.maximum(m_prev, jnp.broadcast_to(jnp.max(m_part, axis=-1, keepdims=True), (rows, dv)))
            alpha = jnp.exp2(m_prev - m_new)
            p = jnp.exp2(s - jnp.concatenate([m_new] * lane_tiles, axis=1)).astype(BF16)
            acc_ref[g] = (jnp.concatenate([alpha, alpha], axis=1) * acc_ref[g]
                          + jnp.dot(p, v_ext, preferred_element_type=F32))
            m_ref[g] = m_new

    def finish():
        for g in range(ATT_GROUPS):
            acc = acc_ref[g]
            o = acc[:, :dv] / acc[:, dv:]
            z = z_ref[g * rows:(g + 1) * rows, :].astype(F32)
            o_ref[g * rows:(g + 1) * rows, :] = (o * _silu(z)).astype(o_ref.dtype)

    m_ref[...] = jnp.full(m_ref.shape, -jnp.inf, F32)
    acc_ref[...] = jnp.zeros(acc_ref.shape, F32)
    scores(sa_ref, 0)

    def pair(jj, carry):
        blk = 2 * jj
        scores(sb_ref, blk + 1)
        softmax_pv(sa_ref, blk, False)
        scores(sa_ref, blk + 2)
        softmax_pv(sb_ref, blk + 1, False)
        return carry

    lax.fori_loop(0, i // 2, pair, 0)

    @pl.when(i % 2 == 0)
    def _():
        softmax_pv(sa_ref, i, True)
        finish()

    @pl.when(i % 2 == 1)
    def _():
        scores(sb_ref, i)
        softmax_pv(sa_ref, i - 1, False)
        softmax_pv(sb_ref, i, True)
        finish()


def _attention(q, k, v, z, tq=ATT_TQ):
    s = q.shape[0]
    tq = min(tq, s)
    tk = tq
    rows = tq // ATT_GROUPS
    return pl.pallas_call(
        functools.partial(_attn_kernel, tk=tk), grid=(HEADS, s // tq),
        in_specs=[pl.BlockSpec((tq, MLA_QK_PAD), lambda h, i: (i, h)),
                  pl.BlockSpec((s, MLA_QK_PAD), lambda h, i: (0, h)),
                  pl.BlockSpec((s, HEAD_DIM), lambda h, i: (0, h)),
                  pl.BlockSpec((tq, HEAD_DIM), lambda h, i: (i, h))],
        out_specs=pl.BlockSpec((tq, HEAD_DIM), lambda h, i: (i, h)),
        out_shape=jax.ShapeDtypeStruct((s, HEADS * HEAD_DIM), BF16),
        scratch_shapes=[pltpu.VMEM((ATT_GROUPS, rows, tk), F32), pltpu.VMEM((ATT_GROUPS, rows, tk), F32),
                        pltpu.VMEM((ATT_GROUPS, rows, HEAD_DIM), F32),
                        pltpu.VMEM((ATT_GROUPS, rows, 2 * HEAD_DIM), F32)],
        compiler_params=_params("parallel", "arbitrary"), name="mla_attention",
    )(q, k, v, z)


def _gdn_branch(h, w_in, w_conv, a_log, dt_bias, norm_g, w_out):
    width = 4 * HEADS * HEAD_DIM
    proj = _matmul(h, w_in[:, :width].astype(BF16), BF16, bm=1024, bn=1024)
    w_gate = jnp.pad(w_in[:, width:], ((0, 0), (0, 128 - 2 * HEADS))).astype(BF16)
    ab = _matmul(h, w_gate, F32, bm=1024, bn=128)[:, :2 * HEADS]
    beta, gc = _gdn_gates(ab.T, a_log, dt_bias, GDN_CHUNK)
    gated = _gdn_core(proj, gc, beta, w_conv, norm_g)
    return _matmul(gated, w_out.astype(BF16), F32, bm=1024, bn=1024)


def _mla_branch(h, positions, w_in, q_norm_g, w_qb, kv_norm_g, w_kvb, w_out):
    s = h.shape[0]
    n_lat = MLA_Q_RANK + MLA_KV_RANK + MLA_ROPE
    lat = _matmul(h, w_in[:, :n_lat].astype(BF16), F32, bm=512, bn=n_lat)
    z = _matmul(h, w_in[:, n_lat:].astype(BF16), BF16, bm=1024, bn=1024)
    cq = lat[:, :MLA_Q_RANK]
    ckv = lat[:, MLA_Q_RANK:MLA_Q_RANK + MLA_KV_RANK]
    k_rope = jnp.pad(lat[:, MLA_Q_RANK + MLA_KV_RANK:], ((0, 0), (HEAD_DIM, MLA_QK_PAD - MLA_QK)))

    wq = w_qb.reshape(MLA_Q_RANK, HEADS, MLA_QK)
    wq = jnp.pad(wq, ((0, 0), (0, 0), (0, MLA_QK_PAD - MLA_QK))).reshape(MLA_Q_RANK, HEADS * MLA_QK_PAD)
    wkv = w_kvb.reshape(MLA_KV_RANK, HEADS, 2 * HEAD_DIM)
    wk = jnp.pad(wkv[:, :, :HEAD_DIM], ((0, 0), (0, 0), (0, MLA_QK_PAD - HEAD_DIM)))
    wk = wk.reshape(MLA_KV_RANK, HEADS * MLA_QK_PAD)
    wv = wkv[:, :, HEAD_DIM:].reshape(MLA_KV_RANK, HEADS * HEAD_DIM)

    cm, sn, sp = _rope_tables(positions.reshape(s))
    q = _qk_proj(_q_proj_kernel, cq, q_norm_g.reshape(1, -1), wq.astype(BF16), (cm, sn, sp))
    k = _qk_proj(_k_proj_kernel, ckv, kv_norm_g.reshape(1, -1), wk.astype(BF16), (k_rope, cm, sn, sp))
    v = _rms_matmul(ckv, kv_norm_g.reshape(1, -1), wv.astype(BF16), BF16, bm=1024, bn=1024)
    gated = _attention(q, k, v, z)
    return _matmul(gated, w_out.astype(BF16), F32, bm=1024, bn=1024)


def kernel(x, c, positions, w_mod, b_mod, ln_g, ln_b, a_w_in, a_w_conv, a_a_log, a_dt_bias, a_norm_g, a_w_out,
           b_w_in, b_q_norm_g, b_w_qb, b_kv_norm_g, b_w_kvb, b_w_out):
    bsz, s, d = x.shape
    assert bsz == 1 and d == D_MODEL and DEPTH == 2
    x0 = x.reshape(s, d)
    mod = _modulation(c, w_mod, b_mod)
    vec = lambda t: t.reshape(1, d)

    h0 = _modulate(x0, mod[0, 1], mod[0, 0])
    y0 = _gdn_branch(h0, a_w_in[0], a_w_conv[0], a_a_log[0], a_dt_bias[0], a_norm_g[0], a_w_out[0])
    x1, h1 = _residual_layernorm(x0, y0, mod[0, 2], vec(ln_g[0]), vec(ln_b[0]), mod[1, 1], mod[1, 0])
    y1 = _mla_branch(h1, positions, b_w_in[0], b_q_norm_g[0], b_w_qb[0], b_kv_norm_g[0], b_w_kvb[0], b_w_out[0])
    x2 = _residual_layernorm(x1, y1, mod[1, 2], vec(ln_g[1]), vec(ln_b[1]))
    return x2.reshape(bsz, s, d)
```

```python
import functools

import jax
import jax.numpy as jnp
from jax import lax
from jax.experimental import pallas as pl
from jax.experimental.pallas import tpu as pltpu

F32 = jnp.float32
BF16 = jnp.bfloat16

D_MODEL = 4096
DEPTH = 2
DEEPNORM_ALPHA = (2.0 * DEPTH) ** 0.25
RMS_EPS = 1e-6
LN_EPS = 1e-5

HEADS = 32
HEAD_DIM = 128
GDN_CONV = 4
GDN_CHUNK = 128
GDN_BLOCK = 1024
GDN_HEADS_PER_STEP = 4
GDN_INV_BASE = 32
CONV_HALO = 8

MLA_Q_RANK = 896
MLA_KV_RANK = 512
MLA_ROPE = 64
MLA_QK = HEAD_DIM + MLA_ROPE
MLA_QK_PAD = 256
ROPE_THETA = 10000.0
LOG2_E = 1.4426950408889634
ATT_TQ = 512
ATT_TK = 512
ATT_GROUPS = 2

VMEM_LIMIT = 56 * 1024 * 1024


def _params(*sem):
    return pltpu.CompilerParams(dimension_semantics=sem, vmem_limit_bytes=VMEM_LIMIT)


def _silu(x):
    return x / (1.0 + jnp.exp(-x))


def _mod_kernel(c_ref, w_ref, b_ref, o_ref):
    tn = o_ref.shape[-1]
    rows = 128

    def body(kk, acc):
        r = pl.multiple_of(kk * rows, rows)
        c = c_ref[pl.ds(r, rows), :]
        a = _silu(c)
        p = w_ref[pl.ds(r, rows), :] * jnp.concatenate([a] * (tn // 128), axis=1)
        for t in range(rows // 8):
            acc = acc + p[t * 8:(t + 1) * 8, :]
        return acc

    acc = lax.fori_loop(0, c_ref.shape[0] // rows, body, jnp.zeros((8, tn), F32))
    o_ref[...] = jnp.sum(acc, axis=0, keepdims=True) + b_ref[...]


def _modulation(c, w_mod, b_mod, tn=512):
    depth, d, n = w_mod.shape
    c_rep = jnp.broadcast_to(c.reshape(d, 1), (d, 128))
    out = pl.pallas_call(
        _mod_kernel,
        grid=(depth, n // tn),
        in_specs=[pl.BlockSpec((d, 128), lambda l, j: (0, 0)),
                  pl.BlockSpec((None, d, tn), lambda l, j: (l, 0, j)),
                  pl.BlockSpec((None, 1, tn), lambda l, j: (l, 0, j))],
        out_specs=pl.BlockSpec((None, 1, tn), lambda l, j: (l, 0, j)),
        out_shape=jax.ShapeDtypeStruct((depth, 1, n), F32),
        compiler_params=_params("parallel", "parallel"),
        name="adaln_mod",
    )(c_rep, w_mod, b_mod.reshape(depth, 1, n))
    return out.reshape(depth, 3, 1, d)


def _modulate_kernel(x_ref, sc_ref, sh_ref, h_ref):
    h_ref[...] = (x_ref[...] * (1.0 + sc_ref[...]) + sh_ref[...]).astype(h_ref.dtype)


def _modulate(x, scale, shift, bm=512):
    s, d = x.shape
    row = pl.BlockSpec((bm, d), lambda i: (i, 0))
    vec = pl.BlockSpec((1, d), lambda i: (0, 0))
    return pl.pallas_call(
        _modulate_kernel, grid=(s // bm,), in_specs=[row, vec, vec], out_specs=row,
        out_shape=jax.ShapeDtypeStruct((s, d), BF16),
        compiler_params=_params("parallel"), name="modulate",
    )(x, scale, shift)


def _resid_ln(x_ref, y_ref, gate_ref, g_ref, b_ref):
    t = DEEPNORM_ALPHA * x_ref[...] + (1.0 + gate_ref[...]) * y_ref[...].astype(F32)
    mu = jnp.mean(t, axis=-1, keepdims=True)
    dlt = t - mu
    var = jnp.mean(dlt * dlt, axis=-1, keepdims=True)
    return dlt * lax.rsqrt(var + LN_EPS) * g_ref[...] + b_ref[...]


def _resid_ln_kernel(x_ref, y_ref, gate_ref, g_ref, b_ref, xo_ref):
    xo_ref[...] = _resid_ln(x_ref, y_ref, gate_ref, g_ref, b_ref)


def _resid_ln_mod_kernel(x_ref, y_ref, gate_ref, g_ref, b_ref, sc_ref, sh_ref, xo_ref, h_ref):
    xn = _resid_ln(x_ref, y_ref, gate_ref, g_ref, b_ref)
    xo_ref[...] = xn
    h_ref[...] = (xn * (1.0 + sc_ref[...]) + sh_ref[...]).astype(h_ref.dtype)


def _residual_layernorm(x, y, gate, ln_g, ln_b, next_scale=None, next_shift=None, bm=256):
    s, d = x.shape
    row = pl.BlockSpec((bm, d), lambda i: (i, 0))
    vec = pl.BlockSpec((1, d), lambda i: (0, 0))
    if next_scale is None:
        return pl.pallas_call(
            _resid_ln_kernel, grid=(s // bm,), in_specs=[row, row, vec, vec, vec], out_specs=row,
            out_shape=jax.ShapeDtypeStruct((s, d), F32),
            compiler_params=_params("parallel"), name="resid_ln",
        )(x, y, gate, ln_g, ln_b)
    return pl.pallas_call(
        _resid_ln_mod_kernel, grid=(s // bm,), in_specs=[row, row, vec, vec, vec, vec, vec],
        out_specs=[row, row],
        out_shape=[jax.ShapeDtypeStruct((s, d), F32), jax.ShapeDtypeStruct((s, d), BF16)],
        compiler_params=_params("parallel"), name="resid_ln_mod",
    )(x, y, gate, ln_g, ln_b, next_scale, next_shift)


def _mm_kernel(a_ref, b_ref, o_ref):
    o_ref[...] = jnp.dot(a_ref[...], b_ref[...], preferred_element_type=F32).astype(o_ref.dtype)


def _matmul(a, b, out_dtype, bm, bn):
    m, k = a.shape
    n = b.shape[1]
    bm = min(bm, m)
    return pl.pallas_call(
        _mm_kernel, grid=(n // bn, m // bm),
        in_specs=[pl.BlockSpec((bm, k), lambda j, i: (i, 0)),
                  pl.BlockSpec((k, bn), lambda j, i: (0, j))],
        out_specs=pl.BlockSpec((bm, bn), lambda j, i: (i, j)),
        out_shape=jax.ShapeDtypeStruct((m, n), out_dtype),
        compiler_params=_params("parallel", "parallel"), name="matmul",
    )(a, b)


def _rms_rows(a, g):
    return a * lax.rsqrt(jnp.mean(a * a, axis=-1, keepdims=True) + RMS_EPS) * g


def _rms_mm_kernel(a_ref, g_ref, b_ref, o_ref):
    an = _rms_rows(a_ref[...], g_ref[...]).astype(BF16)
    o_ref[...] = jnp.dot(an, b_ref[...], preferred_element_type=F32).astype(o_ref.dtype)


def _rms_matmul(a, g, b, out_dtype, bm, bn):
    m, k = a.shape
    n = b.shape[1]
    bm = min(bm, m)
    return pl.pallas_call(
        _rms_mm_kernel, grid=(n // bn, m // bm),
        in_specs=[pl.BlockSpec((bm, k), lambda j, i: (i, 0)),
                  pl.BlockSpec((1, k), lambda j, i: (0, 0)),
                  pl.BlockSpec((k, bn), lambda j, i: (0, j))],
        out_specs=pl.BlockSpec((bm, bn), lambda j, i: (i, j)),
        out_shape=jax.ShapeDtypeStruct((m, n), out_dtype),
        compiler_params=_params("parallel", "parallel"), name="rms_matmul",
    )(a, g, b)


def _gdn_gate_kernel(ab_ref, alog_ref, dtb_ref, beta_ref, gc_ref, *, chunk):
    b_raw = ab_ref[0:HEADS, :]
    a_raw = ab_ref[HEADS:2 * HEADS, :]
    beta_ref[...] = 1.0 / (1.0 + jnp.exp(-b_raw))
    xx = a_raw + dtb_ref[...]
    softplus = jnp.maximum(xx, 0.0) + jnp.log(1.0 + jnp.exp(-jnp.abs(xx)))
    g = -jnp.exp(alog_ref[...]) * softplus
    pos = lax.broadcasted_iota(jnp.int32, g.shape, 1) % chunk
    sh = 1
    while sh < chunk:
        g = g + jnp.where(pos >= sh, pltpu.roll(g, sh, axis=1), 0.0)
        sh *= 2
    gc_ref[...] = g


def _gdn_gates(ab_t, a_log, dt_bias, chunk, tn=2048):
    two_h, s = ab_t.shape
    tn = min(tn, s)
    col = pl.BlockSpec((HEADS, 1), lambda j: (0, 0))
    out = pl.BlockSpec((HEADS, tn), lambda j: (0, j))
    return pl.pallas_call(
        functools.partial(_gdn_gate_kernel, chunk=chunk), grid=(s // tn,),
        in_specs=[pl.BlockSpec((two_h, tn), lambda j: (0, j)), col, col],
        out_specs=[out, out],
        out_shape=[jax.ShapeDtypeStruct((HEADS, s), F32)] * 2,
        compiler_params=_params("parallel"), name="gdn_gates",
    )(ab_t, a_log.reshape(HEADS, 1), dt_bias.reshape(HEADS, 1))


def _bmm(a, b):
    return jnp.einsum("nik,nkj->nij", a.astype(BF16), b.astype(BF16), preferred_element_type=F32)


def _unit_lower_inverse(a_mat, row, col, eye, base):
    size = a_mat.shape[-1]
    same = lambda b: (row // b) == (col // b)
    a0 = jnp.where(same(base), a_mat, 0.0)
    p = eye - a0
    m = _bmm(a0, a0)
    n = 1
    while 2 * n < base // 2:
        pm = _bmm(jnp.concatenate([p, m], axis=1), m)
        p = p + pm[:, :size]
        m = pm[:, size:]
        n += n
    p = p + _bmm(p, m)
    b = base
    while b < size:
        a_off = jnp.where(same(2 * b) & jnp.logical_not(same(b)), a_mat, 0.0)
        p = p - _bmm(_bmm(p, a_off), p)
        b *= 2
    return p


def _gdn_core_kernel(q_ref, k_ref, v_ref, z_ref, gc_ref, beta_ref, wq_ref, wk_ref, wv_ref, ng_ref,
                     o_ref, state_ref, qbuf, kbuf, vbuf, u_s, wq_s, qk_s, kdt_s, gl_s, *, chunk):
    bt = q_ref.shape[0]
    dk = HEAD_DIM
    nh = q_ref.shape[1] // dk
    nc = bt // chunk
    halo = CONV_HALO

    @pl.when(pl.program_id(1) == 0)
    def _():
        state_ref[...] = jnp.zeros_like(state_ref)
        for buf in (qbuf, kbuf, vbuf):
            buf[0:halo, :] = jnp.zeros((halo, nh * dk), F32)

    qbuf[halo:halo + bt, :] = q_ref[...].astype(F32)
    kbuf[halo:halo + bt, :] = k_ref[...].astype(F32)
    vbuf[halo:halo + bt, :] = v_ref[...].astype(F32)

    row = lax.broadcasted_iota(jnp.int32, (chunk, chunk), 0)
    col = lax.broadcasted_iota(jnp.int32, (chunk, chunk), 1)
    incl = row >= col
    strict = row > col
    eye = (row == col).astype(F32)

    def conv_silu(buf, w_ref, lanes):
        ext = buf[0:bt + halo, lanes]
        w = w_ref[:, lanes]
        acc = ext[halo:, :] * w[GDN_CONV - 1:GDN_CONV, :]
        for j in range(GDN_CONV - 1):
            sh = GDN_CONV - 1 - j
            acc = acc + pltpu.roll(ext, sh, axis=0)[halo:, :] * w[j:j + 1, :]
        return _silu(acc)

    def l2n(x):
        return x * lax.rsqrt(jnp.sum(x * x, axis=-1, keepdims=True) + RMS_EPS)

    def chunk_rows(ref, hh):
        tiles = [jnp.broadcast_to(ref[hh:hh + 1, c * chunk:(c + 1) * chunk], (chunk, chunk)) for c in range(nc)]
        return jnp.stack(tiles), jnp.stack([t.T for t in tiles])

    def phase_a(hh):
        lanes = slice(hh * dk, (hh + 1) * dk)
        q = (l2n(conv_silu(qbuf, wq_ref, lanes)) * (dk ** -0.5)).reshape(nc, chunk, dk)
        k = l2n(conv_silu(kbuf, wk_ref, lanes)).reshape(nc, chunk, dk)
        v = conv_silu(vbuf, wv_ref, lanes).reshape(nc, chunk, dk)

        gc_rb, gc_cb = chunk_rows(gc_ref, hh)
        _, bt_cb = chunk_rows(beta_ref, hh)
        decay = jnp.where(incl, jnp.exp(jnp.where(incl, gc_cb - gc_rb, 0.0)), 0.0)

        qkk = jnp.einsum("nid,njd->nij", jnp.concatenate([q, k], axis=1).astype(BF16), k.astype(BF16),
                         preferred_element_type=F32)
        qk = qkk[:, :chunk] * decay
        a_mat = jnp.where(strict, qkk[:, chunk:] * bt_cb * decay, 0.0)
        t_mat = _unit_lower_inverse(a_mat, row, col, eye, GDN_INV_BASE)

        e_gc = jnp.exp(gc_cb)
        uw = _bmm(t_mat, jnp.concatenate([v * bt_cb, k * (bt_cb * e_gc)], axis=2))
        gl_row = gc_cb[:, chunk - 1:chunk, :]
        k_dec = k * jnp.exp(gl_row - gc_cb)

        u_s[:, hh] = uw[:, :, :dk]
        wq_s[:, hh] = jnp.concatenate([uw[:, :, dk:], q * e_gc], axis=1).astype(BF16)
        qk_s[:, hh] = qk.astype(BF16)
        kdt_s[:, hh] = jnp.stack([k_dec[c].T for c in range(nc)]).astype(BF16)
        gl_s[:, hh] = jnp.broadcast_to(jnp.exp(gl_row), (nc, 8, dk))

    for hh in range(nh):
        phase_a(hh)

    def phase_b(c, state):
        ws = _bmm(wq_s[c], state)
        v_new = u_s[c] - ws[:, :chunk]
        vb = v_new.astype(BF16)
        o = ws[:, chunk:] + _bmm(qk_s[c], vb)
        state = state * gl_s[c][:, 0:1, :] + _bmm(kdt_s[c], vb)
        on = o * lax.rsqrt(jnp.mean(o * o, axis=-1, keepdims=True) + RMS_EPS) * ng_ref[...]
        r0 = pl.multiple_of(c * chunk, chunk)
        z = z_ref[pl.ds(r0, chunk), :].astype(F32)
        for hh in range(nh):
            lanes = slice(hh * dk, (hh + 1) * dk)
            o_ref[pl.ds(r0, chunk), lanes] = (on[hh] * _silu(z[:, lanes])).astype(o_ref.dtype)
        return state

    state_ref[...] = lax.fori_loop(0, nc, phase_b, state_ref[...])
    for buf in (qbuf, kbuf, vbuf):
        buf[0:halo, :] = buf[bt:bt + halo, :]


def _gdn_core(proj, gc, beta, w_conv, norm_g, chunk=GDN_CHUNK, bt=GDN_BLOCK, nh=GDN_HEADS_PER_STEP):
    s = proj.shape[0]
    bt = min(bt, s)
    dk = HEAD_DIM
    nc = bt // chunk
    groups = HEADS // nh
    tok = lambda part: pl.BlockSpec((bt, nh * dk), lambda h, b: (b, part * groups + h))
    gate = pl.BlockSpec((None, nh, bt), lambda h, b: (h, 0, b))
    cw = lambda part: pl.BlockSpec((GDN_CONV, nh * dk), lambda h, b: (0, part * groups + h))
    return pl.pallas_call(
        functools.partial(_gdn_core_kernel, chunk=chunk),
        grid=(groups, s // bt),
        in_specs=[tok(0), tok(1), tok(2), tok(3), gate, gate,
                  cw(0), cw(1), cw(2), pl.BlockSpec((1, dk), lambda h, b: (0, 0))],
        out_specs=pl.BlockSpec((bt, nh * dk), lambda h, b: (b, h)),
        out_shape=jax.ShapeDtypeStruct((s, HEADS * dk), BF16),
        scratch_shapes=[pltpu.VMEM((nh, dk, dk), F32)] + [pltpu.VMEM((bt + 2 * CONV_HALO, nh * dk), F32)] * 3
        + [pltpu.VMEM((nc, nh, chunk, dk), F32), pltpu.VMEM((nc, nh, 2 * chunk, dk), BF16),
           pltpu.VMEM((nc, nh, chunk, chunk), BF16), pltpu.VMEM((nc, nh, dk, chunk), BF16),
           pltpu.VMEM((nc, nh, 8, dk), F32)],
        compiler_params=_params("parallel", "arbitrary"), name="gdn_core",
    )(proj, proj, proj, proj, gc.reshape(groups, nh, s), beta.reshape(groups, nh, s),
      w_conv, w_conv, w_conv, norm_g.reshape(1, dk))


def _rope_tab_kernel(pos_ref, invf_ref, cm_ref, sn_ref, sp_ref):
    ang = pos_ref[...].astype(F32) * invf_ref[...]
    cos = jnp.cos(ang)
    sin = jnp.sin(ang)
    lane = lax.broadcasted_iota(jnp.int32, ang.shape, 1)
    half = MLA_ROPE // 2
    rope = (lane >= HEAD_DIM) & (lane < HEAD_DIM + MLA_ROPE)
    cm_ref[...] = jnp.where(lane < HEAD_DIM, 1.0, jnp.where(rope, cos, 0.0))
    sn_ref[...] = jnp.where(rope & (lane < HEAD_DIM + half), -sin, 0.0)
    sp_ref[...] = jnp.where(rope & (lane >= HEAD_DIM + half), sin, 0.0)


def _rope_tables(positions, bm=512):
    s = positions.shape[0]
    bm = min(bm, s)
    half = MLA_ROPE // 2
    inv_freq = ROPE_THETA ** (-jnp.arange(0, half, dtype=F32) / half)
    invf = jnp.concatenate([jnp.zeros((HEAD_DIM,), F32), inv_freq, inv_freq,
                            jnp.zeros((MLA_QK_PAD - MLA_QK,), F32)]).reshape(1, MLA_QK_PAD)
    tab = pl.BlockSpec((bm, MLA_QK_PAD), lambda i: (i, 0))
    return pl.pallas_call(
        _rope_tab_kernel, grid=(s // bm,),
        in_specs=[pl.BlockSpec((bm, 1), lambda i: (i, 0)), pl.BlockSpec((1, MLA_QK_PAD), lambda i: (0, 0))],
        out_specs=[tab, tab, tab],
        out_shape=[jax.ShapeDtypeStruct((s, MLA_QK_PAD), F32)] * 3,
        compiler_params=_params("parallel"), name="rope_tables",
    )(positions.reshape(s, 1), invf)


def _rope(x, cm, sn, sp):
    w = MLA_QK_PAD
    half = MLA_ROPE // 2
    return x * cm + pltpu.roll(x, w - half, axis=1) * sn + pltpu.roll(x, half, axis=1) * sp


def _q_proj_kernel(a_ref, g_ref, w_ref, cm_ref, sn_ref, sp_ref, o_ref):
    an = _rms_rows(a_ref[...], g_ref[...]).astype(BF16)
    acc = jnp.dot(an, w_ref[...], preferred_element_type=F32)
    cm, sn, sp = cm_ref[...], sn_ref[...], sp_ref[...]
    scale = MLA_QK ** -0.5 * LOG2_E
    for g in range(o_ref.shape[1] // MLA_QK_PAD):
        sl = slice(g * MLA_QK_PAD, (g + 1) * MLA_QK_PAD)
        o_ref[:, sl] = (_rope(acc[:, sl], cm, sn, sp) * scale).astype(o_ref.dtype)


def _k_proj_kernel(a_ref, g_ref, w_ref, kr_ref, cm_ref, sn_ref, sp_ref, o_ref):
    an = _rms_rows(a_ref[...], g_ref[...]).astype(BF16)
    acc = jnp.dot(an, w_ref[...], preferred_element_type=F32)
    kr = _rope(kr_ref[...], cm_ref[...], sn_ref[...], sp_ref[...])
    for g in range(o_ref.shape[1] // MLA_QK_PAD):
        sl = slice(g * MLA_QK_PAD, (g + 1) * MLA_QK_PAD)
        o_ref[:, sl] = (acc[:, sl] + kr).astype(o_ref.dtype)


def _qk_proj(kernel, a, g, w, extras, bm=512, bn=2048):
    m, k = a.shape
    n = w.shape[1]
    bm = min(bm, m)
    tab = pl.BlockSpec((bm, MLA_QK_PAD), lambda j, i: (i, 0))
    return pl.pallas_call(
        kernel, grid=(n // bn, m // bm),
        in_specs=[pl.BlockSpec((bm, k), lambda j, i: (i, 0)),
                  pl.BlockSpec((1, k), lambda j, i: (0, 0)),
                  pl.BlockSpec((k, bn), lambda j, i: (0, j))] + [tab] * len(extras),
        out_specs=pl.BlockSpec((bm, bn), lambda j, i: (i, j)),
        out_shape=jax.ShapeDtypeStruct((m, n), BF16),
        compiler_params=_params("parallel", "parallel"), name=kernel.__name__.strip("_"),
    )(a, g, w, *extras)


def _attn_kernel(q_ref, k_ref, v_ref, z_ref, o_ref, sa_ref, sb_ref, m_ref, acc_ref, *, tk):
    i = pl.program_id(1)
    tq = q_ref.shape[0]
    dv = v_ref.shape[1]
    rows = tq // ATT_GROUPS
    lane_tiles = tk // dv

    def scores(s_ref, blk):
        k_blk = k_ref[pl.ds(pl.multiple_of(blk * tk, tk), tk), :]
        for g in range(ATT_GROUPS):
            q = q_ref[g * rows:(g + 1) * rows, :]
            s_ref[g] = lax.dot_general(q, k_blk, (((1,), (1,)), ((), ())), preferred_element_type=F32)

    def softmax_pv(s_ref, blk, masked):
        v_blk = v_ref[pl.ds(pl.multiple_of(blk * tk, tk), tk), :]
        v_ext = jnp.concatenate([v_blk, jnp.ones((tk, dv), BF16)], axis=1)
        for g in range(ATT_GROUPS):
            s = s_ref[g]
            if masked:
                row = lax.broadcasted_iota(jnp.int32, (rows, tk), 0) + g * rows
                col = lax.broadcasted_iota(jnp.int32, (rows, tk), 1)
                s = jnp.where(col <= row, s, -jnp.inf)
            m_part = s[:, :dv]
            for t in range(1, lane_tiles):
                m_part = jnp.maximum(m_part, s[:, t * dv:(t + 1) * dv])
            m_prev = m_ref[g]
            m_new = jnp.maximum(m_prev, jnp.broadcast_to(jnp.max(m_part, axis=-1, keepdims=True), (rows, dv)))
            alpha = jnp.exp2(m_prev - m_new)
            p = jnp.exp2(s - jnp.concatenate([m_new] * lane_tiles, axis=1)).astype(BF16)
            acc_ref[g] = (jnp.concatenate([alpha, alpha], axis=1) * acc_ref[g]
                          + jnp.dot(p, v_ext, preferred_element_type=F32))
            m_ref[g] = m_new

    def finish():
        for g in range(ATT_GROUPS):
            acc = acc_ref[g]
            o = acc[:, :dv] / acc[:, dv:]
            z = z_ref[g * rows:(g + 1) * rows, :].astype(F32)
            o_ref[g * rows:(g + 1) * rows, :] = (o * _silu(z)).astype(o_ref.dtype)

    m_ref[...] = jnp.full(m_ref.shape, -jnp.inf, F32)
    acc_ref[...] = jnp.zeros(acc_ref.shape, F32)
    scores(sa_ref, 0)

    def pair(jj, carry):
        blk = 2 * jj
        scores(sb_ref, blk + 1)
        softmax_pv(sa_ref, blk, False)
        scores(sa_ref, blk + 2)
        softmax_pv(sb_ref, blk + 1, False)
        return carry

    lax.fori_loop(0, i // 2, pair, 0)

    @pl.when(i % 2 == 0)
    def _():
        softmax_pv(sa_ref, i, True)
        finish()

    @pl.when(i % 2 == 1)
    def _():
        scores(sb_ref, i)
        softmax_pv(sa_ref, i - 1, False)
        softmax_pv(sb_ref, i, True)
        finish()


def _attention(q, k, v, z, tq=ATT_TQ):
    s = q.shape[0]
    tq = min(tq, s)
    tk = tq
    rows = tq // ATT_GROUPS
    return pl.pallas_call(
        functools.partial(_attn_kernel, tk=tk), grid=(HEADS, s // tq),
        in_specs=[pl.BlockSpec((tq, MLA_QK_PAD), lambda h, i: (i, h)),
                  pl.BlockSpec((s, MLA_QK_PAD), lambda h, i: (0, h)),
                  pl.BlockSpec((s, HEAD_DIM), lambda h, i: (0, h)),
                  pl.BlockSpec((tq, HEAD_DIM), lambda h, i: (i, h))],
        out_specs=pl.BlockSpec((tq, HEAD_DIM), lambda h, i: (i, h)),
        out_shape=jax.ShapeDtypeStruct((s, HEADS * HEAD_DIM), BF16),
        scratch_shapes=[pltpu.VMEM((ATT_GROUPS, rows, tk), F32), pltpu.VMEM((ATT_GROUPS, rows, tk), F32),
                        pltpu.VMEM((ATT_GROUPS, rows, HEAD_DIM), F32),
                        pltpu.VMEM((ATT_GROUPS, rows, 2 * HEAD_DIM), F32)],
        compiler_params=_params("parallel", "arbitrary"), name="mla_attention",
    )(q, k, v, z)


def _gdn_branch(h, w_in, w_conv, a_log, dt_bias, norm_g, w_out):
    width = 4 * HEADS * HEAD_DIM
    proj = _matmul(h, w_in[:, :width].astype(BF16), BF16, bm=1024, bn=1024)
    w_gate = jnp.pad(w_in[:, width:], ((0, 0), (0, 128 - 2 * HEADS))).astype(BF16)
    ab = _matmul(h, w_gate, F32, bm=1024, bn=128)[:, :2 * HEADS]
    beta, gc = _gdn_gates(ab.T, a_log, dt_bias, GDN_CHUNK)
    gated = _gdn_core(proj, gc, beta, w_conv, norm_g)
    return _matmul(gated, w_out.astype(BF16), F32, bm=1024, bn=1024)


def _mla_branch(h, positions, w_in, q_norm_g, w_qb, kv_norm_g, w_kvb, w_out):
    s = h.shape[0]
    n_lat = MLA_Q_RANK + MLA_KV_RANK + MLA_ROPE
    lat = _matmul(h, w_in[:, :n_lat].astype(BF16), F32, bm=512, bn=n_lat)
    z = _matmul(h, w_in[:, n_lat:].astype(BF16), BF16, bm=1024, bn=1024)
    cq = lat[:, :MLA_Q_RANK]
    ckv = lat[:, MLA_Q_RANK:MLA_Q_RANK + MLA_KV_RANK]
    k_rope = jnp.pad(lat[:, MLA_Q_RANK + MLA_KV_RANK:], ((0, 0), (HEAD_DIM, MLA_QK_PAD - MLA_QK)))

    wq = w_qb.reshape(MLA_Q_RANK, HEADS, MLA_QK)
    wq = jnp.pad(wq, ((0, 0), (0, 0), (0, MLA_QK_PAD - MLA_QK))).reshape(MLA_Q_RANK, HEADS * MLA_QK_PAD)
    wkv = w_kvb.reshape(MLA_KV_RANK, HEADS, 2 * HEAD_DIM)
    wk = jnp.pad(wkv[:, :, :HEAD_DIM], ((0, 0), (0, 0), (0, MLA_QK_PAD - HEAD_DIM)))
    wk = wk.reshape(MLA_KV_RANK, HEADS * MLA_QK_PAD)
    wv = wkv[:, :, HEAD_DIM:].reshape(MLA_KV_RANK, HEADS * HEAD_DIM)

    cm, sn, sp = _rope_tables(positions.reshape(s))
    q = _qk_proj(_q_proj_kernel, cq, q_norm_g.reshape(1, -1), wq.astype(BF16), (cm, sn, sp))
    k = _qk_proj(_k_proj_kernel, ckv, kv_norm_g.reshape(1, -1), wk.astype(BF16), (k_rope, cm, sn, sp))
    v = _rms_matmul(ckv, kv_norm_g.reshape(1, -1), wv.astype(BF16), BF16, bm=1024, bn=1024)
    gated = _attention(q, k, v, z)
    return _matmul(gated, w_out.astype(BF16), F32, bm=1024, bn=1024)


def kernel(x, c, positions, w_mod, b_mod, ln_g, ln_b, a_w_in, a_w_conv, a_a_log, a_dt_bias, a_norm_g, a_w_out,
           b_w_in, b_q_norm_g, b_w_qb, b_kv_norm_g, b_w_kvb, b_w_out):
    bsz, s, d = x.shape
    assert bsz == 1 and d == D_MODEL and DEPTH == 2
    x0 = x.reshape(s, d)
    mod = _modulation(c, w_mod, b_mod)
    vec = lambda t: t.reshape(1, d)

    h0 = _modulate(x0, mod[0, 1], mod[0, 0])
    y0 = _gdn_branch(h0, a_w_in[0], a_w_conv[0], a_a_log[0], a_dt_bias[0], a_norm_g[0], a_w_out[0])
    x1, h1 = _residual_layernorm(x0, y0, mod[0, 2], vec(ln_g[0]), vec(ln_b[0]), mod[1, 1], mod[1, 0])
    y1 = _mla_branch(h1, positions, b_w_in[0], b_q_norm_g[0], b_w_qb[0], b_kv_norm_g[0], b_w_kvb[0], b_w_out[0])
    x2 = _residual_layernorm(x1, y1, mod[1, 2], vec(ln_g[1]), vec(ln_b[1]))
    return x2.reshape(bsz, s, d)
```

```python
import functools

import jax
import jax.numpy as jnp
from jax import lax
from jax.experimental import pallas as pl
from jax.experimental.pallas import tpu as pltpu

F32 = jnp.float32
BF16 = jnp.bfloat16

D_MODEL = 4096
DEPTH = 2
DEEPNORM_ALPHA = (2.0 * DEPTH) ** 0.25
RMS_EPS = 1e-6
LN_EPS = 1e-5

HEADS = 32
HEAD_DIM = 128
GDN_CONV = 4
GDN_CHUNK = 128
GDN_BLOCK = 1024
GDN_HEADS_PER_STEP = 4
GDN_INV_BASE = 32
CONV_HALO = 8

MLA_Q_RANK = 896
MLA_KV_RANK = 512
MLA_ROPE = 64
MLA_QK = HEAD_DIM + MLA_ROPE
MLA_QK_PAD = 256
ROPE_THETA = 10000.0
LOG2_E = 1.4426950408889634
ATT_TQ = 1024
ATT_GROUPS = 4

VMEM_LIMIT = 56 * 1024 * 1024


def _params(*sem):
    return pltpu.CompilerParams(dimension_semantics=sem, vmem_limit_bytes=VMEM_LIMIT)


def _silu(x):
    return x / (1.0 + jnp.exp(-x))


def _mod_kernel(c_ref, w_ref, b_ref, o_ref):
    tn = o_ref.shape[-1]
    rows = 128

    def body(kk, acc):
        r = pl.multiple_of(kk * rows, rows)
        c = c_ref[pl.ds(r, rows), :]
        a = _silu(c)
        p = w_ref[pl.ds(r, rows), :] * jnp.concatenate([a] * (tn // 128), axis=1)
        for t in range(rows // 8):
            acc = acc + p[t * 8:(t + 1) * 8, :]
        return acc

    acc = lax.fori_loop(0, c_ref.shape[0] // rows, body, jnp.zeros((8, tn), F32))
    o_ref[...] = jnp.sum(acc, axis=0, keepdims=True) + b_ref[...]


def _modulation(c, w_mod, b_mod, tn=512):
    depth, d, n = w_mod.shape
    c_rep = jnp.broadcast_to(c.reshape(d, 1), (d, 128))
    out = pl.pallas_call(
        _mod_kernel,
        grid=(depth, n // tn),
        in_specs=[pl.BlockSpec((d, 128), lambda l, j: (0, 0)),
                  pl.BlockSpec((None, d, tn), lambda l, j: (l, 0, j)),
                  pl.BlockSpec((None, 1, tn), lambda l, j: (l, 0, j))],
        out_specs=pl.BlockSpec((None, 1, tn), lambda l, j: (l, 0, j)),
        out_shape=jax.ShapeDtypeStruct((depth, 1, n), F32),
        compiler_params=_params("parallel", "parallel"),
        name="adaln_mod",
    )(c_rep, w_mod, b_mod.reshape(depth, 1, n))
    return out.reshape(depth, 3, 1, d)


def _modulate_kernel(x_ref, sc_ref, sh_ref, h_ref):
    h_ref[...] = (x_ref[...] * (1.0 + sc_ref[...]) + sh_ref[...]).astype(h_ref.dtype)


def _modulate(x, scale, shift, bm=512):
    s, d = x.shape
    row = pl.BlockSpec((bm, d), lambda i: (i, 0))
    vec = pl.BlockSpec((1, d), lambda i: (0, 0))
    return pl.pallas_call(
        _modulate_kernel, grid=(s // bm,), in_specs=[row, vec, vec], out_specs=row,
        out_shape=jax.ShapeDtypeStruct((s, d), BF16),
        compiler_params=_params("parallel"), name="modulate",
    )(x, scale, shift)


def _resid_ln(x_ref, y_ref, gate_ref, g_ref, b_ref):
    t = DEEPNORM_ALPHA * x_ref[...] + (1.0 + gate_ref[...]) * y_ref[...].astype(F32)
    mu = jnp.mean(t, axis=-1, keepdims=True)
    dlt = t - mu
    var = jnp.mean(dlt * dlt, axis=-1, keepdims=True)
    return dlt * lax.rsqrt(var + LN_EPS) * g_ref[...] + b_ref[...]


def _resid_ln_kernel(x_ref, y_ref, gate_ref, g_ref, b_ref, xo_ref):
    xo_ref[...] = _resid_ln(x_ref, y_ref, gate_ref, g_ref, b_ref)


def _resid_ln_mod_kernel(x_ref, y_ref, gate_ref, g_ref, b_ref, sc_ref, sh_ref, xo_ref, h_ref):
    xn = _resid_ln(x_ref, y_ref, gate_ref, g_ref, b_ref)
    xo_ref[...] = xn
    h_ref[...] = (xn * (1.0 + sc_ref[...]) + sh_ref[...]).astype(h_ref.dtype)


def _residual_layernorm(x, y, gate, ln_g, ln_b, next_scale=None, next_shift=None, bm=256):
    s, d = x.shape
    row = pl.BlockSpec((bm, d), lambda i: (i, 0))
    vec = pl.BlockSpec((1, d), lambda i: (0, 0))
    if next_scale is None:
        return pl.pallas_call(
            _resid_ln_kernel, grid=(s // bm,), in_specs=[row, row, vec, vec, vec], out_specs=row,
            out_shape=jax.ShapeDtypeStruct((s, d), F32),
            compiler_params=_params("parallel"), name="resid_ln",
        )(x, y, gate, ln_g, ln_b)
    return pl.pallas_call(
        _resid_ln_mod_kernel, grid=(s // bm,), in_specs=[row, row, vec, vec, vec, vec, vec],
        out_specs=[row, row],
        out_shape=[jax.ShapeDtypeStruct((s, d), F32), jax.ShapeDtypeStruct((s, d), BF16)],
        compiler_params=_params("parallel"), name="resid_ln_mod",
    )(x, y, gate, ln_g, ln_b, next_scale, next_shift)


def _mm_kernel(a_ref, b_ref, o_ref):
    o_ref[...] = jnp.dot(a_ref[...], b_ref[...], preferred_element_type=F32).astype(o_ref.dtype)


def _matmul(a, b, out_dtype, bm, bn):
    m, k = a.shape
    n = b.shape[1]
    bm = min(bm, m)
    return pl.pallas_call(
        _mm_kernel, grid=(n // bn, m // bm),
        in_specs=[pl.BlockSpec((bm, k), lambda j, i: (i, 0)),
                  pl.BlockSpec((k, bn), lambda j, i: (0, j))],
        out_specs=pl.BlockSpec((bm, bn), lambda j, i: (i, j)),
        out_shape=jax.ShapeDtypeStruct((m, n), out_dtype),
        compiler_params=_params("parallel", "parallel"), name="matmul",
    )(a, b)


def _mm_w32_kernel(a_ref, w_ref, o_ref, wb_ref):
    @pl.when(pl.program_id(1) == 0)
    def _():
        wb_ref[...] = w_ref[...].astype(BF16)

    o_ref[...] = jnp.dot(a_ref[...], wb_ref[...], preferred_element_type=F32).astype(o_ref.dtype)


def _matmul_w32(a, w, n, out_dtype, bm, bn):
    m, k = a.shape
    bm = min(bm, m)
    return pl.pallas_call(
        _mm_w32_kernel, grid=(n // bn, m // bm),
        in_specs=[pl.BlockSpec((bm, k), lambda j, i: (i, 0)),
                  pl.BlockSpec((k, bn), lambda j, i: (0, j))],
        out_specs=pl.BlockSpec((bm, bn), lambda j, i: (i, j)),
        out_shape=jax.ShapeDtypeStruct((m, n), out_dtype),
        scratch_shapes=[pltpu.VMEM((k, bn), BF16)],
        compiler_params=_params("parallel", "arbitrary"), name="matmul_w32",
    )(a, w)


def _rms_rows(a, g):
    return a * lax.rsqrt(jnp.mean(a * a, axis=-1, keepdims=True) + RMS_EPS) * g


def _rms_mm_kernel(a_ref, g_ref, b_ref, o_ref):
    an = _rms_rows(a_ref[...], g_ref[...]).astype(BF16)
    o_ref[...] = jnp.dot(an, b_ref[...], preferred_element_type=F32).astype(o_ref.dtype)


def _rms_matmul(a, g, b, out_dtype, bm, bn):
    m, k = a.shape
    n = b.shape[1]
    bm = min(bm, m)
    return pl.pallas_call(
        _rms_mm_kernel, grid=(n // bn, m // bm),
        in_specs=[pl.BlockSpec((bm, k), lambda j, i: (i, 0)),
                  pl.BlockSpec((1, k), lambda j, i: (0, 0)),
                  pl.BlockSpec((k, bn), lambda j, i: (0, j))],
        out_specs=pl.BlockSpec((bm, bn), lambda j, i: (i, j)),
        out_shape=jax.ShapeDtypeStruct((m, n), out_dtype),
        compiler_params=_params("parallel", "parallel"), name="rms_matmul",
    )(a, g, b)


def _gdn_gate_kernel(ab_ref, alog_ref, dtb_ref, beta_ref, gc_ref, *, chunk):
    b_raw = ab_ref[0:HEADS, :]
    a_raw = ab_ref[HEADS:2 * HEADS, :]
    beta_ref[...] = 1.0 / (1.0 + jnp.exp(-b_raw))
    xx = a_raw + dtb_ref[...]
    softplus = jnp.maximum(xx, 0.0) + jnp.log(1.0 + jnp.exp(-jnp.abs(xx)))
    g = -jnp.exp(alog_ref[...]) * softplus
    pos = lax.broadcasted_iota(jnp.int32, g.shape, 1) % chunk
    sh = 1
    while sh < chunk:
        g = g + jnp.where(pos >= sh, pltpu.roll(g, sh, axis=1), 0.0)
        sh *= 2
    gc_ref[...] = g


def _gdn_gates(ab_t, a_log, dt_bias, chunk, tn=2048):
    two_h, s = ab_t.shape
    tn = min(tn, s)
    col = pl.BlockSpec((HEADS, 1), lambda j: (0, 0))
    out = pl.BlockSpec((HEADS, tn), lambda j: (0, j))
    return pl.pallas_call(
        functools.partial(_gdn_gate_kernel, chunk=chunk), grid=(s // tn,),
        in_specs=[pl.BlockSpec((two_h, tn), lambda j: (0, j)), col, col],
        out_specs=[out, out],
        out_shape=[jax.ShapeDtypeStruct((HEADS, s), F32)] * 2,
        compiler_params=_params("parallel"), name="gdn_gates",
    )(ab_t, a_log.reshape(HEADS, 1), dt_bias.reshape(HEADS, 1))


def _bmm(a, b):
    return jnp.einsum("nik,nkj->nij", a.astype(BF16), b.astype(BF16), preferred_element_type=F32)


def _unit_lower_inverse(a_mat, row, col, eye, base):
    size = a_mat.shape[-1]
    same = lambda b: (row // b) == (col // b)
    a0 = jnp.where(same(base), a_mat, 0.0)
    p = eye - a0
    m = _bmm(a0, a0)
    n = 1
    while 2 * n < base // 2:
        pm = _bmm(jnp.concatenate([p, m], axis=1), m)
        p = p + pm[:, :size]
        m = pm[:, size:]
        n += n
    p = p + _bmm(p, m)
    b = base
    while b < size:
        a_off = jnp.where(same(2 * b) & jnp.logical_not(same(b)), a_mat, 0.0)
        p = p - _bmm(_bmm(p, a_off), p)
        b *= 2
    return p


def _gdn_core_kernel(q_ref, k_ref, v_ref, z_ref, gc_ref, beta_ref, wq_ref, wk_ref, wv_ref, ng_ref,
                     o_ref, state_ref, qbuf, kbuf, vbuf, u_s, wq_s, qk_s, kdt_s, gl_s, *, chunk):
    bt = q_ref.shape[0]
    dk = HEAD_DIM
    nh = q_ref.shape[1] // dk
    nc = bt // chunk
    halo = CONV_HALO

    @pl.when(pl.program_id(1) == 0)
    def _():
        state_ref[...] = jnp.zeros_like(state_ref)
        for buf in (qbuf, kbuf, vbuf):
            buf[0:halo, :] = jnp.zeros((halo, nh * dk), F32)

    qbuf[halo:halo + bt, :] = q_ref[...].astype(F32)
    kbuf[halo:halo + bt, :] = k_ref[...].astype(F32)
    vbuf[halo:halo + bt, :] = v_ref[...].astype(F32)

    row = lax.broadcasted_iota(jnp.int32, (chunk, chunk), 0)
    col = lax.broadcasted_iota(jnp.int32, (chunk, chunk), 1)
    incl = row >= col
    strict = row > col
    eye = (row == col).astype(F32)

    def conv_silu(buf, w_ref, lanes):
        ext = buf[0:bt + halo, lanes]
        w = w_ref[:, lanes]
        acc = ext[halo:, :] * w[GDN_CONV - 1:GDN_CONV, :]
        for j in range(GDN_CONV - 1):
            sh = GDN_CONV - 1 - j
            acc = acc + pltpu.roll(ext, sh, axis=0)[halo:, :] * w[j:j + 1, :]
        return _silu(acc)

    def l2n(x):
        return x * lax.rsqrt(jnp.sum(x * x, axis=-1, keepdims=True) + RMS_EPS)

    def chunk_rows(ref, hh):
        tiles = [jnp.broadcast_to(ref[hh:hh + 1, c * chunk:(c + 1) * chunk], (chunk, chunk)) for c in range(nc)]
        return jnp.stack(tiles), jnp.stack([t.T for t in tiles])

    def phase_a(hh):
        lanes = slice(hh * dk, (hh + 1) * dk)
        q = (l2n(conv_silu(qbuf, wq_ref, lanes)) * (dk ** -0.5)).reshape(nc, chunk, dk)
        k = l2n(conv_silu(kbuf, wk_ref, lanes)).reshape(nc, chunk, dk)
        v = conv_silu(vbuf, wv_ref, lanes).reshape(nc, chunk, dk)

        gc_rb, gc_cb = chunk_rows(gc_ref, hh)
        _, bt_cb = chunk_rows(beta_ref, hh)
        decay = jnp.where(incl, jnp.exp(jnp.where(incl, gc_cb - gc_rb, 0.0)), 0.0)

        qkk = jnp.einsum("nid,njd->nij", jnp.concatenate([q, k], axis=1).astype(BF16), k.astype(BF16),
                         preferred_element_type=F32)
        qk = qkk[:, :chunk] * decay
        a_mat = jnp.where(strict, qkk[:, chunk:] * bt_cb * decay, 0.0)
        t_mat = _unit_lower_inverse(a_mat, row, col, eye, GDN_INV_BASE)

        e_gc = jnp.exp(gc_cb)
        uw = _bmm(t_mat, jnp.concatenate([v * bt_cb, k * (bt_cb * e_gc)], axis=2))
        gl_row = gc_cb[:, chunk - 1:chunk, :]
        k_dec = k * jnp.exp(gl_row - gc_cb)

        u_s[:, hh] = uw[:, :, :dk]
        wq_s[:, hh] = jnp.concatenate([uw[:, :, dk:], q * e_gc], axis=1).astype(BF16)
        qk_s[:, hh] = qk.astype(BF16)
        kdt_s[:, hh] = jnp.stack([k_dec[c].T for c in range(nc)]).astype(BF16)
        gl_s[:, hh] = jnp.broadcast_to(jnp.exp(gl_row), (nc, 8, dk))

    for hh in range(nh):
        phase_a(hh)

    def phase_b(c, state):
        ws = _bmm(wq_s[c], state)
        v_new = u_s[c] - ws[:, :chunk]
        vb = v_new.astype(BF16)
        o = ws[:, chunk:] + _bmm(qk_s[c], vb)
        state = state * gl_s[c][:, 0:1, :] + _bmm(kdt_s[c], vb)
        on = o * lax.rsqrt(jnp.mean(o * o, axis=-1, keepdims=True) + RMS_EPS) * ng_ref[...]
        r0 = pl.multiple_of(c * chunk, chunk)
        z = z_ref[pl.ds(r0, chunk), :].astype(F32)
        for hh in range(nh):
            lanes = slice(hh * dk, (hh + 1) * dk)
            o_ref[pl.ds(r0, chunk), lanes] = (on[hh] * _silu(z[:, lanes])).astype(o_ref.dtype)
        return state

    state_ref[...] = lax.fori_loop(0, nc, phase_b, state_ref[...])
    for buf in (qbuf, kbuf, vbuf):
        buf[0:halo, :] = buf[bt:bt + halo, :]


def _gdn_core(proj, gc, beta, w_conv, norm_g, chunk=GDN_CHUNK, bt=GDN_BLOCK, nh=GDN_HEADS_PER_STEP):
    s = proj.shape[0]
    bt = min(bt, s)
    dk = HEAD_DIM
    nc = bt // chunk
    groups = HEADS // nh
    tok = lambda part: pl.BlockSpec((bt, nh * dk), lambda h, b: (b, part * groups + h))
    gate = pl.BlockSpec((None, nh, bt), lambda h, b: (h, 0, b))
    cw = lambda part: pl.BlockSpec((GDN_CONV, nh * dk), lambda h, b: (0, part * groups + h))
    return pl.pallas_call(
        functools.partial(_gdn_core_kernel, chunk=chunk),
        grid=(groups, s // bt),
        in_specs=[tok(0), tok(1), tok(2), tok(3), gate, gate,
                  cw(0), cw(1), cw(2), pl.BlockSpec((1, dk), lambda h, b: (0, 0))],
        out_specs=pl.BlockSpec((bt, nh * dk), lambda h, b: (b, h)),
        out_shape=jax.ShapeDtypeStruct((s, HEADS * dk), BF16),
        scratch_shapes=[pltpu.VMEM((nh, dk, dk), F32)] + [pltpu.VMEM((bt + 2 * CONV_HALO, nh * dk), F32)] * 3
        + [pltpu.VMEM((nc, nh, chunk, dk), F32), pltpu.VMEM((nc, nh, 2 * chunk, dk), BF16),
           pltpu.VMEM((nc, nh, chunk, chunk), BF16), pltpu.VMEM((nc, nh, dk, chunk), BF16),
           pltpu.VMEM((nc, nh, 8, dk), F32)],
        compiler_params=_params("parallel", "arbitrary"), name="gdn_core",
    )(proj, proj, proj, proj, gc.reshape(groups, nh, s), beta.reshape(groups, nh, s),
      w_conv, w_conv, w_conv, norm_g.reshape(1, dk))


def _rope_tab_kernel(pos_ref, invf_ref, cm_ref, sn_ref, sp_ref):
    ang = pos_ref[...].astype(F32) * invf_ref[...]
    cos = jnp.cos(ang)
    sin = jnp.sin(ang)
    lane = lax.broadcasted_iota(jnp.int32, ang.shape, 1)
    half = MLA_ROPE // 2
    cm_ref[...] = jnp.where(lane < MLA_ROPE, cos, 0.0)
    sn_ref[...] = jnp.where(lane < half, -sin, 0.0)
    sp_ref[...] = jnp.where((lane >= half) & (lane < MLA_ROPE), sin, 0.0)


def _rope_tables(positions, bm=512):
    s = positions.shape[0]
    bm = min(bm, s)
    half = MLA_ROPE // 2
    inv_freq = ROPE_THETA ** (-jnp.arange(0, half, dtype=F32) / half)
    invf = jnp.concatenate([inv_freq, inv_freq, jnp.zeros((HEAD_DIM - MLA_ROPE,), F32)]).reshape(1, HEAD_DIM)
    tab = pl.BlockSpec((bm, HEAD_DIM), lambda i: (i, 0))
    return pl.pallas_call(
        _rope_tab_kernel, grid=(s // bm,),
        in_specs=[pl.BlockSpec((bm, 1), lambda i: (i, 0)), pl.BlockSpec((1, HEAD_DIM), lambda i: (0, 0))],
        out_specs=[tab, tab, tab],
        out_shape=[jax.ShapeDtypeStruct((s, HEAD_DIM), F32)] * 3,
        compiler_params=_params("parallel"), name="rope_tables",
    )(positions.reshape(s, 1), invf)


def _rope(x, cm, sn, sp):
    half = MLA_ROPE // 2
    return x * cm + pltpu.roll(x, HEAD_DIM - half, axis=1) * sn + pltpu.roll(x, half, axis=1) * sp


def _q_proj_kernel(a_ref, g_ref, w_ref, cm_ref, sn_ref, sp_ref, o_ref):
    an = _rms_rows(a_ref[...], g_ref[...]).astype(BF16)
    acc = jnp.dot(an, w_ref[...], preferred_element_type=F32)
    scale = MLA_QK ** -0.5 * LOG2_E
    cm, sn, sp = cm_ref[...] * scale, sn_ref[...] * scale, sp_ref[...] * scale
    for g in range(o_ref.shape[1] // MLA_QK_PAD):
        lo = g * MLA_QK_PAD
        hi = lo + HEAD_DIM
        o_ref[:, lo:hi] = (acc[:, lo:hi] * scale).astype(o_ref.dtype)
        o_ref[:, hi:hi + HEAD_DIM] = _rope(acc[:, hi:hi + HEAD_DIM], cm, sn, sp).astype(o_ref.dtype)


def _k_proj_kernel(a_ref, g_ref, w_ref, kr_ref, cm_ref, sn_ref, sp_ref, o_ref):
    an = _rms_rows(a_ref[...], g_ref[...]).astype(BF16)
    acc = jnp.dot(an, w_ref[...], preferred_element_type=F32)
    kr = _rope(kr_ref[...], cm_ref[...], sn_ref[...], sp_ref[...]).astype(o_ref.dtype)
    for g in range(o_ref.shape[1] // MLA_QK_PAD):
        lo = g * MLA_QK_PAD
        o_ref[:, lo:lo + HEAD_DIM] = acc[:, g * HEAD_DIM:(g + 1) * HEAD_DIM].astype(o_ref.dtype)
        o_ref[:, lo + HEAD_DIM:lo + MLA_QK_PAD] = kr


def _qk_proj(kernel, a, g, w, extras, bm=512, heads_per_step=8):
    m, k = a.shape
    bm = min(bm, m)
    bn = heads_per_step * MLA_QK_PAD
    bw = w.shape[1] // (HEADS // heads_per_step)
    tab = pl.BlockSpec((bm, HEAD_DIM), lambda j, i: (i, 0))
    return pl.pallas_call(
        kernel, grid=(HEADS // heads_per_step, m // bm),
        in_specs=[pl.BlockSpec((bm, k), lambda j, i: (i, 0)),
                  pl.BlockSpec((1, k), lambda j, i: (0, 0)),
                  pl.BlockSpec((k, bw), lambda j, i: (0, j))] + [tab] * len(extras),
        out_specs=pl.BlockSpec((bm, bn), lambda j, i: (i, j)),
        out_shape=jax.ShapeDtypeStruct((m, HEADS * MLA_QK_PAD), BF16),
        compiler_params=_params("parallel", "parallel"), name=kernel.__name__.strip("_"),
    )(a, g, w, *extras)


def _attn_kernel(q_ref, k_ref, v_ref, z_ref, o_ref, sa_ref, sb_ref, m_ref, acc_ref, *, tk):
    i = pl.program_id(1)
    tq = q_ref.shape[0]
    dv = v_ref.shape[1]
    rows = tq // ATT_GROUPS

    def scores(s_ref, blk):
        k_blk = k_ref[pl.ds(pl.multiple_of(blk * tk, tk), tk), :]
        s_ref[...] = lax.dot_general(q_ref[...], k_blk, (((1,), (1,)), ((), ())), preferred_element_type=F32)

    def softmax_pv(s_ref, blk, masked):
        start = pl.multiple_of(blk * tk, tk)
        for g in range(ATT_GROUPS):
            width = (g + 1) * rows if masked else tk
            tiles = width // dv
            s = s_ref[g * rows:(g + 1) * rows, 0:width]
            if masked:
                row = lax.broadcasted_iota(jnp.int32, (rows, width), 0) + g * rows
                col = lax.broadcasted_iota(jnp.int32, (rows, width), 1)
                s = jnp.where(col <= row, s, -jnp.inf)
            m_part = s[:, :dv]
            for t in range(1, tiles):
                m_part = jnp.maximum(m_part, s[:, t * dv:(t + 1) * dv])
            m_prev = m_ref[g]
            m_new = jnp.maximum(m_prev, jnp.broadcast_to(jnp.max(m_part, axis=-1, keepdims=True), (rows, dv)))
            alpha = jnp.exp2(m_prev - m_new)
            p = jnp.exp2(s - jnp.concatenate([m_new] * tiles, axis=1)).astype(BF16)
            v_ext = jnp.concatenate([v_ref[pl.ds(start, width), :], jnp.ones((width, dv), BF16)], axis=1)
            acc_ref[g] = (jnp.concatenate([alpha, alpha], axis=1) * acc_ref[g]
                          + jnp.dot(p, v_ext, preferred_element_type=F32))
            m_ref[g] = m_new

    def finish():
        for g in range(ATT_GROUPS):
            acc = acc_ref[g]
            o = acc[:, :dv] / acc[:, dv:]
            z = z_ref[g * rows:(g + 1) * rows, :].astype(F32)
            o_ref[g * rows:(g + 1) * rows, :] = (o * _silu(z)).astype(o_ref.dtype)

    m_ref[...] = jnp.full(m_ref.shape, -jnp.inf, F32)
    acc_ref[...] = jnp.zeros(acc_ref.shape, F32)
    scores(sa_ref, 0)

    def pair(jj, carry):
        blk = 2 * jj
        scores(sb_ref, blk + 1)
        softmax_pv(sa_ref, blk, False)
        scores(sa_ref, blk + 2)
        softmax_pv(sb_ref, blk + 1, False)
        return carry

    lax.fori_loop(0, i // 2, pair, 0)

    @pl.when(i % 2 == 0)
    def _():
        softmax_pv(sa_ref, i, True)
        finish()

    @pl.when(i % 2 == 1)
    def _():
        scores(sb_ref, i)
        softmax_pv(sa_ref, i - 1, False)
        softmax_pv(sb_ref, i, True)
        finish()


def _attention(q, k, v, z, tq=ATT_TQ):
    s = q.shape[0]
    tq = min(tq, s)
    tk = tq
    rows = tq // ATT_GROUPS
    return pl.pallas_call(
        functools.partial(_attn_kernel, tk=tk), grid=(HEADS, s // tq),
        in_specs=[pl.BlockSpec((tq, MLA_QK_PAD), lambda h, i: (i, h)),
                  pl.BlockSpec((s, MLA_QK_PAD), lambda h, i: (0, h)),
                  pl.BlockSpec((s, HEAD_DIM), lambda h, i: (0, h)),
                  pl.BlockSpec((tq, HEAD_DIM), lambda h, i: (i, h))],
        out_specs=pl.BlockSpec((tq, HEAD_DIM), lambda h, i: (i, h)),
        out_shape=jax.ShapeDtypeStruct((s, HEADS * HEAD_DIM), BF16),
        scratch_shapes=[pltpu.VMEM((tq, tk), F32), pltpu.VMEM((tq, tk), F32),
                        pltpu.VMEM((ATT_GROUPS, rows, HEAD_DIM), F32),
                        pltpu.VMEM((ATT_GROUPS, rows, 2 * HEAD_DIM), F32)],
        compiler_params=_params("parallel", "arbitrary"), name="mla_attention",
    )(q, k, v, z)


def _gdn_branch(h, w_in, w_conv, a_log, dt_bias, norm_g, w_out):
    width = 4 * HEADS * HEAD_DIM
    proj = _matmul_w32(h, w_in, width, BF16, bm=1024, bn=512)
    w_gate = jnp.pad(w_in[:, width:], ((0, 0), (0, 128 - 2 * HEADS))).astype(BF16)
    ab = _matmul(h, w_gate, F32, bm=1024, bn=128)[:, :2 * HEADS]
    beta, gc = _gdn_gates(ab.T, a_log, dt_bias, GDN_CHUNK)
    gated = _gdn_core(proj, gc, beta, w_conv, norm_g)
    return _matmul_w32(gated, w_out, w_out.shape[1], BF16, bm=1024, bn=512)


def _mla_branch(h, positions, w_in, q_norm_g, w_qb, kv_norm_g, w_kvb, w_out):
    s = h.shape[0]
    n_lat = MLA_Q_RANK + MLA_KV_RANK + MLA_ROPE
    lat = _matmul(h, w_in[:, :n_lat].astype(BF16), F32, bm=512, bn=n_lat)
    z = _matmul(h, w_in[:, n_lat:].astype(BF16), BF16, bm=1024, bn=1024)
    cq = lat[:, :MLA_Q_RANK]
    ckv = lat[:, MLA_Q_RANK:MLA_Q_RANK + MLA_KV_RANK]
    k_rope = jnp.pad(lat[:, MLA_Q_RANK + MLA_KV_RANK:], ((0, 0), (0, HEAD_DIM - MLA_ROPE)))

    wq = w_qb.reshape(MLA_Q_RANK, HEADS, MLA_QK)
    wq = jnp.pad(wq, ((0, 0), (0, 0), (0, MLA_QK_PAD - MLA_QK))).reshape(MLA_Q_RANK, HEADS * MLA_QK_PAD)
    wkv = w_kvb.reshape(MLA_KV_RANK, HEADS, 2 * HEAD_DIM)
    wk = wkv[:, :, :HEAD_DIM].reshape(MLA_KV_RANK, HEADS * HEAD_DIM)
    wv = wkv[:, :, HEAD_DIM:].reshape(MLA_KV_RANK, HEADS * HEAD_DIM)

    cm, sn, sp = _rope_tables(positions.reshape(s))
    q = _qk_proj(_q_proj_kernel, cq, q_norm_g.reshape(1, -1), wq.astype(BF16), (cm, sn, sp))
    k = _qk_proj(_k_proj_kernel, ckv, kv_norm_g.reshape(1, -1), wk.astype(BF16), (k_rope, cm, sn, sp))
    v = _rms_matmul(ckv, kv_norm_g.reshape(1, -1), wv.astype(BF16), BF16, bm=1024, bn=1024)
    gated = _attention(q, k, v, z)
    return _matmul_w32(gated, w_out, w_out.shape[1], BF16, bm=1024, bn=512)


def kernel(x, c, positions, w_mod, b_mod, ln_g, ln_b, a_w_in, a_w_conv, a_a_log, a_dt_bias, a_norm_g, a_w_out,
           b_w_in, b_q_norm_g, b_w_qb, b_kv_norm_g, b_w_kvb, b_w_out):
    bsz, s, d = x.shape
    assert bsz == 1 and d == D_MODEL and DEPTH == 2
    x0 = x.reshape(s, d)
    mod = _modulation(c, w_mod, b_mod)
    vec = lambda t: t.reshape(1, d)

    h0 = _modulate(x0, mod[0, 1], mod[0, 0])
    y0 = _gdn_branch(h0, a_w_in[0], a_w_conv[0], a_a_log[0], a_dt_bias[0], a_norm_g[0], a_w_out[0])
    x1, h1 = _residual_layernorm(x0, y0, mod[0, 2], vec(ln_g[0]), vec(ln_b[0]), mod[1, 1], mod[1, 0])
    y1 = _mla_branch(h1, positions, b_w_in[0], b_q_norm_g[0], b_w_qb[0], b_kv_norm_g[0], b_w_kvb[0], b_w_out[0])
    x2 = _residual_layernorm(x1, y1, mod[1, 2], vec(ln_g[1]), vec(ln_b[1]))
    return x2.reshape(bsz, s, d)
```

```python
import functools

import jax
import jax.numpy as jnp
from jax import lax
from jax.experimental import pallas as pl
from jax.experimental.pallas import tpu as pltpu

F32 = jnp.float32
BF16 = jnp.bfloat16

D_MODEL = 4096
DEPTH = 2
DEEPNORM_ALPHA = (2.0 * DEPTH) ** 0.25
RMS_EPS = 1e-6
LN_EPS = 1e-5

HEADS = 32
HEAD_DIM = 128
GDN_CONV = 4
GDN_CHUNK = 128
GDN_BLOCK = 1024
GDN_HEADS_PER_STEP = 4
GDN_INV_BASE = 32
CONV_HALO = 8

MLA_Q_RANK = 896
MLA_KV_RANK = 512
MLA_ROPE = 64
MLA_QK = HEAD_DIM + MLA_ROPE
MLA_QK_PAD = 256
ROPE_THETA = 10000.0
LOG2_E = 1.4426950408889634
ATT_TQ = 1024
ATT_GROUPS = 4

VMEM_LIMIT = 56 * 1024 * 1024


def _params(*sem):
    return pltpu.CompilerParams(dimension_semantics=sem, vmem_limit_bytes=VMEM_LIMIT)


def _silu(x):
    return x / (1.0 + jnp.exp(-x))


def _mod_kernel(c_ref, w_ref, b_ref, o_ref):
    tn = o_ref.shape[-1]
    rows = 128

    def body(kk, acc):
        r = pl.multiple_of(kk * rows, rows)
        c = c_ref[pl.ds(r, rows), :]
        a = _silu(c)
        p = w_ref[pl.ds(r, rows), :] * jnp.concatenate([a] * (tn // 128), axis=1)
        for t in range(rows // 8):
            acc = acc + p[t * 8:(t + 1) * 8, :]
        return acc

    acc = lax.fori_loop(0, c_ref.shape[0] // rows, body, jnp.zeros((8, tn), F32))
    o_ref[...] = jnp.sum(acc, axis=0, keepdims=True) + b_ref[...]


def _modulation(c, w_mod, b_mod, tn=512):
    depth, d, n = w_mod.shape
    c_rep = jnp.broadcast_to(c.reshape(d, 1), (d, 128))
    out = pl.pallas_call(
        _mod_kernel,
        grid=(depth, n // tn),
        in_specs=[pl.BlockSpec((d, 128), lambda l, j: (0, 0)),
                  pl.BlockSpec((None, d, tn), lambda l, j: (l, 0, j)),
                  pl.BlockSpec((None, 1, tn), lambda l, j: (l, 0, j))],
        out_specs=pl.BlockSpec((None, 1, tn), lambda l, j: (l, 0, j)),
        out_shape=jax.ShapeDtypeStruct((depth, 1, n), F32),
        compiler_params=_params("parallel", "parallel"),
        name="adaln_mod",
    )(c_rep, w_mod, b_mod.reshape(depth, 1, n))
    return out.reshape(depth, 3, 1, d)


def _modulate_kernel(x_ref, sc_ref, sh_ref, h_ref):
    h_ref[...] = (x_ref[...] * (1.0 + sc_ref[...]) + sh_ref[...]).astype(h_ref.dtype)


def _modulate(x, scale, shift, bm=512):
    s, d = x.shape
    row = pl.BlockSpec((bm, d), lambda i: (i, 0))
    vec = pl.BlockSpec((1, d), lambda i: (0, 0))
    return pl.pallas_call(
        _modulate_kernel, grid=(s // bm,), in_specs=[row, vec, vec], out_specs=row,
        out_shape=jax.ShapeDtypeStruct((s, d), BF16),
        compiler_params=_params("parallel"), name="modulate",
    )(x, scale, shift)


def _resid_ln(x_ref, y_ref, gate_ref, g_ref, b_ref):
    t = DEEPNORM_ALPHA * x_ref[...] + (1.0 + gate_ref[...]) * y_ref[...].astype(F32)
    mu = jnp.mean(t, axis=-1, keepdims=True)
    dlt = t - mu
    var = jnp.mean(dlt * dlt, axis=-1, keepdims=True)
    return dlt * lax.rsqrt(var + LN_EPS) * g_ref[...] + b_ref[...]


def _resid_ln_kernel(x_ref, y_ref, gate_ref, g_ref, b_ref, xo_ref):
    xo_ref[...] = _resid_ln(x_ref, y_ref, gate_ref, g_ref, b_ref)


def _resid_ln_mod_kernel(x_ref, y_ref, gate_ref, g_ref, b_ref, sc_ref, sh_ref, xo_ref, h_ref):
    xn = _resid_ln(x_ref, y_ref, gate_ref, g_ref, b_ref)
    xo_ref[...] = xn
    h_ref[...] = (xn * (1.0 + sc_ref[...]) + sh_ref[...]).astype(h_ref.dtype)


def _residual_layernorm(x, y, gate, ln_g, ln_b, next_scale=None, next_shift=None, bm=256):
    s, d = x.shape
    row = pl.BlockSpec((bm, d), lambda i: (i, 0))
    vec = pl.BlockSpec((1, d), lambda i: (0, 0))
    if next_scale is None:
        return pl.pallas_call(
            _resid_ln_kernel, grid=(s // bm,), in_specs=[row, row, vec, vec, vec], out_specs=row,
            out_shape=jax.ShapeDtypeStruct((s, d), F32),
            compiler_params=_params("parallel"), name="resid_ln",
        )(x, y, gate, ln_g, ln_b)
    return pl.pallas_call(
        _resid_ln_mod_kernel, grid=(s // bm,), in_specs=[row, row, vec, vec, vec, vec, vec],
        out_specs=[row, row],
        out_shape=[jax.ShapeDtypeStruct((s, d), F32), jax.ShapeDtypeStruct((s, d), BF16)],
        compiler_params=_params("parallel"), name="resid_ln_mod",
    )(x, y, gate, ln_g, ln_b, next_scale, next_shift)


_NT_DIMS = (((1,), (1,)), ((), ()))


def _dot(a, b, nt):
    if nt:
        return lax.dot_general(a, b, _NT_DIMS, preferred_element_type=F32)
    return jnp.dot(a, b, preferred_element_type=F32)


def _weight_spec(k, bn, nt):
    if nt:
        return pl.BlockSpec((bn, k), lambda j, i: (j, 0))
    return pl.BlockSpec((k, bn), lambda j, i: (0, j))


def _mm_kernel(a_ref, b_ref, o_ref, *, nt):
    o_ref[...] = _dot(a_ref[...], b_ref[...], nt).astype(o_ref.dtype)


def _matmul(a, b, out_dtype, bm, bn, nt=False):
    m, k = a.shape
    n = b.shape[0] if nt else b.shape[1]
    bm = min(bm, m)
    return pl.pallas_call(
        functools.partial(_mm_kernel, nt=nt), grid=(n // bn, m // bm),
        in_specs=[pl.BlockSpec((bm, k), lambda j, i: (i, 0)), _weight_spec(k, bn, nt)],
        out_specs=pl.BlockSpec((bm, bn), lambda j, i: (i, j)),
        out_shape=jax.ShapeDtypeStruct((m, n), out_dtype),
        compiler_params=_params("parallel", "parallel"), name="matmul",
    )(a, b)


def _mm_w32_kernel(a_ref, w_ref, o_ref, wb_ref, *, nt):
    @pl.when(pl.program_id(1) == 0)
    def _():
        wb_ref[...] = w_ref[...].astype(BF16)

    o_ref[...] = _dot(a_ref[...], wb_ref[...], nt).astype(o_ref.dtype)


def _matmul_w32(a, w, n, out_dtype, bm, bn, nt=False):
    m, k = a.shape
    bm = min(bm, m)
    return pl.pallas_call(
        functools.partial(_mm_w32_kernel, nt=nt), grid=(n // bn, m // bm),
        in_specs=[pl.BlockSpec((bm, k), lambda j, i: (i, 0)), _weight_spec(k, bn, nt)],
        out_specs=pl.BlockSpec((bm, bn), lambda j, i: (i, j)),
        out_shape=jax.ShapeDtypeStruct((m, n), out_dtype),
        scratch_shapes=[pltpu.VMEM((bn, k) if nt else (k, bn), BF16)],
        compiler_params=_params("parallel", "arbitrary"), name="matmul_w32",
    )(a, w)


def _rms_rows(a, g):
    return a * lax.rsqrt(jnp.mean(a * a, axis=-1, keepdims=True) + RMS_EPS) * g


def _rms_mm_kernel(a_ref, g_ref, b_ref, o_ref):
    an = _rms_rows(a_ref[...], g_ref[...]).astype(BF16)
    o_ref[...] = jnp.dot(an, b_ref[...], preferred_element_type=F32).astype(o_ref.dtype)


def _rms_matmul(a, g, b, out_dtype, bm, bn):
    m, k = a.shape
    n = b.shape[1]
    bm = min(bm, m)
    return pl.pallas_call(
        _rms_mm_kernel, grid=(n // bn, m // bm),
        in_specs=[pl.BlockSpec((bm, k), lambda j, i: (i, 0)),
                  pl.BlockSpec((1, k), lambda j, i: (0, 0)),
                  pl.BlockSpec((k, bn), lambda j, i: (0, j))],
        out_specs=pl.BlockSpec((bm, bn), lambda j, i: (i, j)),
        out_shape=jax.ShapeDtypeStruct((m, n), out_dtype),
        compiler_params=_params("parallel", "parallel"), name="rms_matmul",
    )(a, g, b)


def _gdn_gate_kernel(ab_ref, alog_ref, dtb_ref, beta_ref, gc_ref, *, chunk):
    b_raw = ab_ref[0:HEADS, :]
    a_raw = ab_ref[HEADS:2 * HEADS, :]
    beta_ref[...] = 1.0 / (1.0 + jnp.exp(-b_raw))
    xx = a_raw + dtb_ref[...]
    softplus = jnp.maximum(xx, 0.0) + jnp.log(1.0 + jnp.exp(-jnp.abs(xx)))
    g = -jnp.exp(alog_ref[...]) * softplus
    pos = lax.broadcasted_iota(jnp.int32, g.shape, 1) % chunk
    sh = 1
    while sh < chunk:
        g = g + jnp.where(pos >= sh, pltpu.roll(g, sh, axis=1), 0.0)
        sh *= 2
    gc_ref[...] = g


def _gdn_gates(ab_t, a_log, dt_bias, chunk, tn=2048):
    two_h, s = ab_t.shape
    tn = min(tn, s)
    col = pl.BlockSpec((HEADS, 1), lambda j: (0, 0))
    out = pl.BlockSpec((HEADS, tn), lambda j: (0, j))
    return pl.pallas_call(
        functools.partial(_gdn_gate_kernel, chunk=chunk), grid=(s // tn,),
        in_specs=[pl.BlockSpec((two_h, tn), lambda j: (0, j)), col, col],
        out_specs=[out, out],
        out_shape=[jax.ShapeDtypeStruct((HEADS, s), F32)] * 2,
        compiler_params=_params("parallel"), name="gdn_gates",
    )(ab_t, a_log.reshape(HEADS, 1), dt_bias.reshape(HEADS, 1))


def _bmm(a, b):
    return jnp.einsum("nik,nkj->nij", a.astype(BF16), b.astype(BF16), preferred_element_type=F32)


def _unit_lower_inverse(a_mat, row, col, eye, base):
    size = a_mat.shape[-1]
    same = lambda b: (row // b) == (col // b)
    a0 = jnp.where(same(base), a_mat, 0.0)
    p = eye - a0
    m = _bmm(a0, a0)
    n = 1
    while 2 * n < base // 2:
        pm = _bmm(jnp.concatenate([p, m], axis=1), m)
        p = p + pm[:, :size]
        m = pm[:, size:]
        n += n
    p = p + _bmm(p, m)
    b = base
    while b < size:
        a_off = jnp.where(same(2 * b) & jnp.logical_not(same(b)), a_mat, 0.0)
        p = p - _bmm(_bmm(p, a_off), p)
        b *= 2
    return p


def _gdn_core_kernel(q_ref, k_ref, v_ref, z_ref, gc_ref, beta_ref, wq_ref, wk_ref, wv_ref, ng_ref,
                     o_ref, state_ref, qbuf, kbuf, vbuf, u_s, wq_s, qk_s, kdt_s, gl_s, *, chunk):
    bt = q_ref.shape[0]
    dk = HEAD_DIM
    nh = q_ref.shape[1] // dk
    nc = bt // chunk
    halo = CONV_HALO

    @pl.when(pl.program_id(1) == 0)
    def _():
        state_ref[...] = jnp.zeros_like(state_ref)
        for buf in (qbuf, kbuf, vbuf):
            buf[0:halo, :] = jnp.zeros((halo, nh * dk), F32)

    qbuf[halo:halo + bt, :] = q_ref[...].astype(F32)
    kbuf[halo:halo + bt, :] = k_ref[...].astype(F32)
    vbuf[halo:halo + bt, :] = v_ref[...].astype(F32)

    row = lax.broadcasted_iota(jnp.int32, (chunk, chunk), 0)
    col = lax.broadcasted_iota(jnp.int32, (chunk, chunk), 1)
    incl = row >= col
    strict = row > col
    eye = (row == col).astype(F32)

    def conv_silu(buf, w_ref, lanes):
        w = w_ref[:, lanes]
        acc = buf[halo:halo + bt, lanes] * w[GDN_CONV - 1:GDN_CONV, :]
        for j in range(GDN_CONV - 1):
            sh = GDN_CONV - 1 - j
            acc = acc + buf[halo - sh:halo - sh + bt, lanes] * w[j:j + 1, :]
        return _silu(acc)

    def l2n(x):
        return x * lax.rsqrt(jnp.sum(x * x, axis=-1, keepdims=True) + RMS_EPS)

    def chunk_rows(ref, hh):
        tiles = [jnp.broadcast_to(ref[hh:hh + 1, c * chunk:(c + 1) * chunk], (chunk, chunk)) for c in range(nc)]
        return jnp.stack(tiles), jnp.stack([t.T for t in tiles])

    def phase_a(hh):
        lanes = slice(hh * dk, (hh + 1) * dk)
        q = (l2n(conv_silu(qbuf, wq_ref, lanes)) * (dk ** -0.5)).reshape(nc, chunk, dk)
        k = l2n(conv_silu(kbuf, wk_ref, lanes)).reshape(nc, chunk, dk)
        v = conv_silu(vbuf, wv_ref, lanes).reshape(nc, chunk, dk)

        gc_rb, gc_cb = chunk_rows(gc_ref, hh)
        _, bt_cb = chunk_rows(beta_ref, hh)
        decay = jnp.where(incl, jnp.exp(jnp.where(incl, gc_cb - gc_rb, 0.0)), 0.0)

        qkk = jnp.einsum("nid,njd->nij", jnp.concatenate([q, k], axis=1).astype(BF16), k.astype(BF16),
                         preferred_element_type=F32)
        qk = qkk[:, :chunk] * decay
        a_mat = jnp.where(strict, qkk[:, chunk:] * bt_cb * decay, 0.0)
        t_mat = _unit_lower_inverse(a_mat, row, col, eye, GDN_INV_BASE)

        e_gc = jnp.exp(gc_cb)
        uw = _bmm(t_mat, jnp.concatenate([v * bt_cb, k * (bt_cb * e_gc)], axis=2))
        gl_row = gc_cb[:, chunk - 1:chunk, :]
        k_dec = k * jnp.exp(gl_row - gc_cb)

        u_s[:, hh] = uw[:, :, :dk]
        wq_s[:, hh] = jnp.concatenate([uw[:, :, dk:], q * e_gc], axis=1).astype(BF16)
        qk_s[:, hh] = qk.astype(BF16)
        kdt_s[:, hh] = jnp.stack([k_dec[c].T for c in range(nc)]).astype(BF16)
        gl_s[:, hh] = jnp.broadcast_to(jnp.exp(gl_row), (nc, 8, dk))

    for hh in range(nh):
        phase_a(hh)

    def phase_b(c, state):
        ws = _bmm(wq_s[c], state)
        v_new = u_s[c] - ws[:, :chunk]
        vb = v_new.astype(BF16)
        o = ws[:, chunk:] + _bmm(qk_s[c], vb)
        state = state * gl_s[c][:, 0:1, :] + _bmm(kdt_s[c], vb)
        on = o * lax.rsqrt(jnp.mean(o * o, axis=-1, keepdims=True) + RMS_EPS) * ng_ref[...]
        r0 = pl.multiple_of(c * chunk, chunk)
        z = z_ref[pl.ds(r0, chunk), :].astype(F32)
        for hh in range(nh):
            lanes = slice(hh * dk, (hh + 1) * dk)
            o_ref[pl.ds(r0, chunk), lanes] = (on[hh] * _silu(z[:, lanes])).astype(o_ref.dtype)
        return state

    state_ref[...] = lax.fori_loop(0, nc, phase_b, state_ref[...])
    for buf in (qbuf, kbuf, vbuf):
        buf[0:halo, :] = buf[bt:bt + halo, :]


def _gdn_core(proj, gc, beta, w_conv, norm_g, chunk=GDN_CHUNK, bt=GDN_BLOCK, nh=GDN_HEADS_PER_STEP):
    s = proj.shape[0]
    bt = min(bt, s)
    dk = HEAD_DIM
    nc = bt // chunk
    groups = HEADS // nh
    tok = lambda part: pl.BlockSpec((bt, nh * dk), lambda h, b: (b, part * groups + h))
    gate = pl.BlockSpec((None, nh, bt), lambda h, b: (h, 0, b))
    cw = lambda part: pl.BlockSpec((GDN_CONV, nh * dk), lambda h, b: (0, part * groups + h))
    return pl.pallas_call(
        functools.partial(_gdn_core_kernel, chunk=chunk),
        grid=(groups, s // bt),
        in_specs=[tok(0), tok(1), tok(2), tok(3), gate, gate,
                  cw(0), cw(1), cw(2), pl.BlockSpec((1, dk), lambda h, b: (0, 0))],
        out_specs=pl.BlockSpec((bt, nh * dk), lambda h, b: (b, h)),
        out_shape=jax.ShapeDtypeStruct((s, HEADS * dk), BF16),
        scratch_shapes=[pltpu.VMEM((nh, dk, dk), F32)] + [pltpu.VMEM((bt + 2 * CONV_HALO, nh * dk), F32)] * 3
        + [pltpu.VMEM((nc, nh, chunk, dk), F32), pltpu.VMEM((nc, nh, 2 * chunk, dk), BF16),
           pltpu.VMEM((nc, nh, chunk, chunk), BF16), pltpu.VMEM((nc, nh, dk, chunk), BF16),
           pltpu.VMEM((nc, nh, 8, dk), F32)],
        compiler_params=_params("parallel", "arbitrary"), name="gdn_core",
    )(proj, proj, proj, proj, gc.reshape(groups, nh, s), beta.reshape(groups, nh, s),
      w_conv, w_conv, w_conv, norm_g.reshape(1, dk))


def _rope_tab_kernel(pos_ref, invf_ref, cm_ref, sn_ref, sp_ref):
    ang = pos_ref[...].astype(F32) * invf_ref[...]
    cos = jnp.cos(ang)
    sin = jnp.sin(ang)
    lane = lax.broadcasted_iota(jnp.int32, ang.shape, 1)
    half = MLA_ROPE // 2
    cm_ref[...] = jnp.where(lane < MLA_ROPE, cos, 0.0)
    sn_ref[...] = jnp.where(lane < half, -sin, 0.0)
    sp_ref[...] = jnp.where((lane >= half) & (lane < MLA_ROPE), sin, 0.0)


def _rope_tables(positions, bm=512):
    s = positions.shape[0]
    bm = min(bm, s)
    half = MLA_ROPE // 2
    inv_freq = ROPE_THETA ** (-jnp.arange(0, half, dtype=F32) / half)
    invf = jnp.concatenate([inv_freq, inv_freq, jnp.zeros((HEAD_DIM - MLA_ROPE,), F32)]).reshape(1, HEAD_DIM)
    tab = pl.BlockSpec((bm, HEAD_DIM), lambda i: (i, 0))
    return pl.pallas_call(
        _rope_tab_kernel, grid=(s // bm,),
        in_specs=[pl.BlockSpec((bm, 1), lambda i: (i, 0)), pl.BlockSpec((1, HEAD_DIM), lambda i: (0, 0))],
        out_specs=[tab, tab, tab],
        out_shape=[jax.ShapeDtypeStruct((s, HEAD_DIM), F32)] * 3,
        compiler_params=_params("parallel"), name="rope_tables",
    )(positions.reshape(s, 1), invf)


def _rope(x, cm, sn, sp):
    half = MLA_ROPE // 2
    return x * cm + pltpu.roll(x, HEAD_DIM - half, axis=1) * sn + pltpu.roll(x, half, axis=1) * sp


def _q_proj_kernel(a_ref, g_ref, w_ref, cm_ref, sn_ref, sp_ref, o_ref):
    an = _rms_rows(a_ref[...], g_ref[...]).astype(BF16)
    acc = jnp.dot(an, w_ref[...], preferred_element_type=F32)
    scale = MLA_QK ** -0.5 * LOG2_E
    cm, sn, sp = cm_ref[...] * scale, sn_ref[...] * scale, sp_ref[...] * scale
    for g in range(o_ref.shape[1] // MLA_QK_PAD):
        lo = g * MLA_QK_PAD
        hi = lo + HEAD_DIM
        o_ref[:, lo:hi] = (acc[:, lo:hi] * scale).astype(o_ref.dtype)
        o_ref[:, hi:hi + HEAD_DIM] = _rope(acc[:, hi:hi + HEAD_DIM], cm, sn, sp).astype(o_ref.dtype)


def _k_proj_kernel(a_ref, g_ref, w_ref, kr_ref, cm_ref, sn_ref, sp_ref, o_ref):
    an = _rms_rows(a_ref[...], g_ref[...]).astype(BF16)
    acc = jnp.dot(an, w_ref[...], preferred_element_type=F32)
    kr = _rope(kr_ref[...], cm_ref[...], sn_ref[...], sp_ref[...]).astype(o_ref.dtype)
    for g in range(o_ref.shape[1] // MLA_QK_PAD):
        lo = g * MLA_QK_PAD
        o_ref[:, lo:lo + HEAD_DIM] = acc[:, g * HEAD_DIM:(g + 1) * HEAD_DIM].astype(o_ref.dtype)
        o_ref[:, lo + HEAD_DIM:lo + MLA_QK_PAD] = kr


def _qk_proj(kernel, a, g, w, extras, bm=1024, heads_per_step=8):
    m, k = a.shape
    bm = min(bm, m)
    bn = heads_per_step * MLA_QK_PAD
    bw = w.shape[1] // (HEADS // heads_per_step)
    tab = pl.BlockSpec((bm, HEAD_DIM), lambda j, i: (i, 0))
    return pl.pallas_call(
        kernel, grid=(HEADS // heads_per_step, m // bm),
        in_specs=[pl.BlockSpec((bm, k), lambda j, i: (i, 0)),
                  pl.BlockSpec((1, k), lambda j, i: (0, 0)),
                  pl.BlockSpec((k, bw), lambda j, i: (0, j))] + [tab] * len(extras),
        out_specs=pl.BlockSpec((bm, bn), lambda j, i: (i, j)),
        out_shape=jax.ShapeDtypeStruct((m, HEADS * MLA_QK_PAD), BF16),
        compiler_params=_params("parallel", "parallel"), name=kernel.__name__.strip("_"),
    )(a, g, w, *extras)


def _attn_kernel(q_ref, k_ref, v_ref, z_ref, o_ref, sa_ref, sb_ref, m_ref, acc_ref, *, tk):
    i = pl.program_id(1)
    tq = q_ref.shape[0]
    dv = v_ref.shape[1]
    rows = tq // ATT_GROUPS

    def scores(s_ref, blk):
        k_blk = k_ref[pl.ds(pl.multiple_of(blk * tk, tk), tk), :]
        s_ref[...] = lax.dot_general(q_ref[...], k_blk, (((1,), (1,)), ((), ())), preferred_element_type=F32)

    def softmax_pv(s_ref, blk, masked):
        start = pl.multiple_of(blk * tk, tk)
        for g in range(ATT_GROUPS):
            width = (g + 1) * rows if masked else tk
            tiles = width // dv
            s = s_ref[g * rows:(g + 1) * rows, 0:width]
            if masked:
                row = lax.broadcasted_iota(jnp.int32, (rows, width), 0) + g * rows
                col = lax.broadcasted_iota(jnp.int32, (rows, width), 1)
                s = jnp.where(col <= row, s, -jnp.inf)
            m_part = s[:, :dv]
            for t in range(1, tiles):
                m_part = jnp.maximum(m_part, s[:, t * dv:(t + 1) * dv])
            m_prev = m_ref[g]
            m_new = jnp.maximum(m_prev, jnp.broadcast_to(jnp.max(m_part, axis=-1, keepdims=True), (rows, dv)))
            alpha = jnp.exp2(m_prev - m_new)
            p = jnp.exp2(s - jnp.concatenate([m_new] * tiles, axis=1)).astype(BF16)
            v_ext = jnp.concatenate([v_ref[pl.ds(start, width), :], jnp.ones((width, dv), BF16)], axis=1)
            acc_ref[g] = (jnp.concatenate([alpha, alpha], axis=1) * acc_ref[g]
                          + jnp.dot(p, v_ext, preferred_element_type=F32))
            m_ref[g] = m_new

    def finish():
        for g in range(ATT_GROUPS):
            acc = acc_ref[g]
            o = acc[:, :dv] / acc[:, dv:]
            z = z_ref[g * rows:(g + 1) * rows, :].astype(F32)
            o_ref[g * rows:(g + 1) * rows, :] = (o * _silu(z)).astype(o_ref.dtype)

    m_ref[...] = jnp.full(m_ref.shape, -jnp.inf, F32)
    acc_ref[...] = jnp.zeros(acc_ref.shape, F32)
    scores(sa_ref, 0)

    def pair(jj, carry):
        blk = 2 * jj
        scores(sb_ref, blk + 1)
        softmax_pv(sa_ref, blk, False)
        scores(sa_ref, blk + 2)
        softmax_pv(sb_ref, blk + 1, False)
        return carry

    lax.fori_loop(0, i // 2, pair, 0)

    @pl.when(i % 2 == 0)
    def _():
        softmax_pv(sa_ref, i, True)
        finish()

    @pl.when(i % 2 == 1)
    def _():
        scores(sb_ref, i)
        softmax_pv(sa_ref, i - 1, False)
        softmax_pv(sb_ref, i, True)
        finish()


def _attention(q, k, v, z, tq=ATT_TQ):
    s = q.shape[0]
    tq = min(tq, s)
    tk = tq
    rows = tq // ATT_GROUPS
    return pl.pallas_call(
        functools.partial(_attn_kernel, tk=tk), grid=(HEADS, s // tq),
        in_specs=[pl.BlockSpec((tq, MLA_QK_PAD), lambda h, i: (i, h)),
                  pl.BlockSpec((s, MLA_QK_PAD), lambda h, i: (0, h)),
                  pl.BlockSpec((s, HEAD_DIM), lambda h, i: (0, h)),
                  pl.BlockSpec((tq, HEAD_DIM), lambda h, i: (i, h))],
        out_specs=pl.BlockSpec((tq, HEAD_DIM), lambda h, i: (i, h)),
        out_shape=jax.ShapeDtypeStruct((s, HEADS * HEAD_DIM), BF16),
        scratch_shapes=[pltpu.VMEM((tq, tk), F32), pltpu.VMEM((tq, tk), F32),
                        pltpu.VMEM((ATT_GROUPS, rows, HEAD_DIM), F32),
                        pltpu.VMEM((ATT_GROUPS, rows, 2 * HEAD_DIM), F32)],
        compiler_params=_params("parallel", "arbitrary"), name="mla_attention",
    )(q, k, v, z)


def _gdn_branch(h, w_in, w_conv, a_log, dt_bias, norm_g, w_out):
    width = 4 * HEADS * HEAD_DIM
    w_in_t = w_in.T
    proj = _matmul_w32(h, w_in_t, width, BF16, bm=1024, bn=512, nt=True)
    w_gate = jnp.pad(w_in_t[width:], ((0, 128 - 2 * HEADS), (0, 0))).astype(BF16)
    ab = _matmul(h, w_gate, F32, bm=1024, bn=128, nt=True)[:, :2 * HEADS]
    beta, gc = _gdn_gates(ab.T, a_log, dt_bias, GDN_CHUNK)
    gated = _gdn_core(proj, gc, beta, w_conv, norm_g)
    return _matmul_w32(gated, w_out, w_out.shape[1], BF16, bm=1024, bn=512)


def _mla_branch(h, positions, w_in, q_norm_g, w_qb, kv_norm_g, w_kvb, w_out):
    s = h.shape[0]
    n_lat = MLA_Q_RANK + MLA_KV_RANK + MLA_ROPE
    w_in_t = w_in.T
    lat = _matmul(h, w_in_t[:n_lat].astype(BF16), F32, bm=512, bn=n_lat, nt=True)
    z = _matmul(h, w_in_t[n_lat:].astype(BF16), BF16, bm=1024, bn=1024, nt=True)
    cq = lat[:, :MLA_Q_RANK]
    ckv = lat[:, MLA_Q_RANK:MLA_Q_RANK + MLA_KV_RANK]
    k_rope = jnp.pad(lat[:, MLA_Q_RANK + MLA_KV_RANK:], ((0, 0), (0, HEAD_DIM - MLA_ROPE)))

    wq = w_qb.reshape(MLA_Q_RANK, HEADS, MLA_QK)
    wq = jnp.pad(wq, ((0, 0), (0, 0), (0, MLA_QK_PAD - MLA_QK))).reshape(MLA_Q_RANK, HEADS * MLA_QK_PAD)
    wkv = w_kvb.reshape(MLA_KV_RANK, HEADS, 2 * HEAD_DIM)
    wk = wkv[:, :, :HEAD_DIM].reshape(MLA_KV_RANK, HEADS * HEAD_DIM)
    wv = wkv[:, :, HEAD_DIM:].reshape(MLA_KV_RANK, HEADS * HEAD_DIM)

    cm, sn, sp = _rope_tables(positions.reshape(s))
    q = _qk_proj(_q_proj_kernel, cq, q_norm_g.reshape(1, -1), wq.astype(BF16), (cm, sn, sp))
    k = _qk_proj(_k_proj_kernel, ckv, kv_norm_g.reshape(1, -1), wk.astype(BF16), (k_rope, cm, sn, sp))
    v = _rms_matmul(ckv, kv_norm_g.reshape(1, -1), wv.astype(BF16), BF16, bm=1024, bn=1024)
    gated = _attention(q, k, v, z)
    return _matmul_w32(gated, w_out, w_out.shape[1], BF16, bm=1024, bn=512)


def kernel(x, c, positions, w_mod, b_mod, ln_g, ln_b, a_w_in, a_w_conv, a_a_log, a_dt_bias, a_norm_g, a_w_out,
           b_w_in, b_q_norm_g, b_w_qb, b_kv_norm_g, b_w_kvb, b_w_out):
    bsz, s, d = x.shape
    assert bsz == 1 and d == D_MODEL and DEPTH == 2
    x0 = x.reshape(s, d)
    mod = _modulation(c, w_mod, b_mod)
    vec = lambda t: t.reshape(1, d)

    h0 = _modulate(x0, mod[0, 1], mod[0, 0])
    y0 = _gdn_branch(h0, a_w_in[0], a_w_conv[0], a_a_log[0], a_dt_bias[0], a_norm_g[0], a_w_out[0])
    x1, h1 = _residual_layernorm(x0, y0, mod[0, 2], vec(ln_g[0]), vec(ln_b[0]), mod[1, 1], mod[1, 0])
    y1 = _mla_branch(h1, positions, b_w_in[0], b_q_norm_g[0], b_w_qb[0], b_kv_norm_g[0], b_w_kvb[0], b_w_out[0])
    x2 = _residual_layernorm(x1, y1, mod[1, 2], vec(ln_g[1]), vec(ln_b[1]))
    return x2.reshape(bsz, s, d)
```

```python
import functools

import jax
import jax.numpy as jnp
from jax import lax
from jax.experimental import pallas as pl
from jax.experimental.pallas import tpu as pltpu

F32 = jnp.float32
BF16 = jnp.bfloat16

D_MODEL = 4096
DEPTH = 2
DEEPNORM_ALPHA = (2.0 * DEPTH) ** 0.25
RMS_EPS = 1e-6
LN_EPS = 1e-5

HEADS = 32
HEAD_DIM = 128
GDN_CONV = 4
GDN_CHUNK = 128
GDN_BLOCK = 1024
GDN_HEADS_PER_STEP = 4
GDN_INV_BASE = 32
CONV_HALO = 8

MLA_Q_RANK = 896
MLA_KV_RANK = 512
MLA_ROPE = 64
MLA_QK = HEAD_DIM + MLA_ROPE
MLA_QK_PAD = 256
ROPE_THETA = 10000.0
LOG2_E = 1.4426950408889634
ATT_TQ = 1024
ATT_GROUPS = 4

VMEM_LIMIT = 56 * 1024 * 1024


def _params(*sem):
    return pltpu.CompilerParams(dimension_semantics=sem, vmem_limit_bytes=VMEM_LIMIT)


def _silu(x):
    return x / (1.0 + jnp.exp(-x))


def _mod_kernel(c_ref, w_ref, b_ref, o_ref):
    tn = o_ref.shape[-1]
    rows = 128

    def body(kk, acc):
        r = pl.multiple_of(kk * rows, rows)
        c = c_ref[pl.ds(r, rows), :]
        a = _silu(c)
        p = w_ref[pl.ds(r, rows), :] * jnp.concatenate([a] * (tn // 128), axis=1)
        for t in range(rows // 8):
            acc = acc + p[t * 8:(t + 1) * 8, :]
        return acc

    acc = lax.fori_loop(0, c_ref.shape[0] // rows, body, jnp.zeros((8, tn), F32))
    o_ref[...] = jnp.sum(acc, axis=0, keepdims=True) + b_ref[...]


def _modulation(c, w_mod, b_mod, tn=512):
    depth, d, n = w_mod.shape
    c_rep = jnp.broadcast_to(c.reshape(d, 1), (d, 128))
    out = pl.pallas_call(
        _mod_kernel,
        grid=(depth, n // tn),
        in_specs=[pl.BlockSpec((d, 128), lambda l, j: (0, 0)),
                  pl.BlockSpec((None, d, tn), lambda l, j: (l, 0, j)),
                  pl.BlockSpec((None, 1, tn), lambda l, j: (l, 0, j))],
        out_specs=pl.BlockSpec((None, 1, tn), lambda l, j: (l, 0, j)),
        out_shape=jax.ShapeDtypeStruct((depth, 1, n), F32),
        compiler_params=_params("parallel", "parallel"),
        name="adaln_mod",
    )(c_rep, w_mod, b_mod.reshape(depth, 1, n))
    return out.reshape(depth, 3, 1, d)


def _modulate_kernel(x_ref, sc_ref, sh_ref, h_ref):
    h_ref[...] = (x_ref[...] * (1.0 + sc_ref[...]) + sh_ref[...]).astype(h_ref.dtype)


def _modulate(x, scale, shift, bm=512):
    s, d = x.shape
    row = pl.BlockSpec((bm, d), lambda i: (i, 0))
    vec = pl.BlockSpec((1, d), lambda i: (0, 0))
    return pl.pallas_call(
        _modulate_kernel, grid=(s // bm,), in_specs=[row, vec, vec], out_specs=row,
        out_shape=jax.ShapeDtypeStruct((s, d), BF16),
        compiler_params=_params("parallel"), name="modulate",
    )(x, scale, shift)


def _resid_ln(x_ref, y_ref, gate_ref, g_ref, b_ref):
    t = DEEPNORM_ALPHA * x_ref[...] + (1.0 + gate_ref[...]) * y_ref[...].astype(F32)
    mu = jnp.mean(t, axis=-1, keepdims=True)
    dlt = t - mu
    var = jnp.mean(dlt * dlt, axis=-1, keepdims=True)
    return dlt * lax.rsqrt(var + LN_EPS) * g_ref[...] + b_ref[...]


def _resid_ln_kernel(x_ref, y_ref, gate_ref, g_ref, b_ref, xo_ref):
    xo_ref[...] = _resid_ln(x_ref, y_ref, gate_ref, g_ref, b_ref)


def _resid_ln_mod_kernel(x_ref, y_ref, gate_ref, g_ref, b_ref, sc_ref, sh_ref, xo_ref, h_ref):
    xn = _resid_ln(x_ref, y_ref, gate_ref, g_ref, b_ref)
    xo_ref[...] = xn
    h_ref[...] = (xn * (1.0 + sc_ref[...]) + sh_ref[...]).astype(h_ref.dtype)


def _residual_layernorm(x, y, gate, ln_g, ln_b, next_scale=None, next_shift=None, bm=256):
    s, d = x.shape
    row = pl.BlockSpec((bm, d), lambda i: (i, 0))
    vec = pl.BlockSpec((1, d), lambda i: (0, 0))
    if next_scale is None:
        return pl.pallas_call(
            _resid_ln_kernel, grid=(s // bm,), in_specs=[row, row, vec, vec, vec], out_specs=row,
            out_shape=jax.ShapeDtypeStruct((s, d), F32),
            compiler_params=_params("parallel"), name="resid_ln",
        )(x, y, gate, ln_g, ln_b)
    return pl.pallas_call(
        _resid_ln_mod_kernel, grid=(s // bm,), in_specs=[row, row, vec, vec, vec, vec, vec],
        out_specs=[row, row],
        out_shape=[jax.ShapeDtypeStruct((s, d), F32), jax.ShapeDtypeStruct((s, d), BF16)],
        compiler_params=_params("parallel"), name="resid_ln_mod",
    )(x, y, gate, ln_g, ln_b, next_scale, next_shift)


_NT_DIMS = (((1,), (1,)), ((), ()))


def _dot(a, b, nt):
    if nt:
        return lax.dot_general(a, b, _NT_DIMS, preferred_element_type=F32)
    return jnp.dot(a, b, preferred_element_type=F32)


def _weight_spec(k, bn, nt):
    if nt:
        return pl.BlockSpec((bn, k), lambda j, i: (j, 0))
    return pl.BlockSpec((k, bn), lambda j, i: (0, j))


def _mm_kernel(a_ref, b_ref, o_ref, *, nt):
    o_ref[...] = _dot(a_ref[...], b_ref[...], nt).astype(o_ref.dtype)


def _matmul(a, b, out_dtype, bm, bn, nt=False):
    m, k = a.shape
    n = b.shape[0] if nt else b.shape[1]
    bm = min(bm, m)
    return pl.pallas_call(
        functools.partial(_mm_kernel, nt=nt), grid=(n // bn, m // bm),
        in_specs=[pl.BlockSpec((bm, k), lambda j, i: (i, 0)), _weight_spec(k, bn, nt)],
        out_specs=pl.BlockSpec((bm, bn), lambda j, i: (i, j)),
        out_shape=jax.ShapeDtypeStruct((m, n), out_dtype),
        compiler_params=_params("parallel", "parallel"), name="matmul",
    )(a, b)


def _mm_w32_kernel(a_ref, w_ref, o_ref, wb_ref, *, nt):
    @pl.when(pl.program_id(1) == 0)
    def _():
        wb_ref[...] = w_ref[...].astype(BF16)

    o_ref[...] = _dot(a_ref[...], wb_ref[...], nt).astype(o_ref.dtype)


def _matmul_w32(a, w, n, out_dtype, bm, bn, nt=False):
    m, k = a.shape
    bm = min(bm, m)
    return pl.pallas_call(
        functools.partial(_mm_w32_kernel, nt=nt), grid=(n // bn, m // bm),
        in_specs=[pl.BlockSpec((bm, k), lambda j, i: (i, 0)), _weight_spec(k, bn, nt)],
        out_specs=pl.BlockSpec((bm, bn), lambda j, i: (i, j)),
        out_shape=jax.ShapeDtypeStruct((m, n), out_dtype),
        scratch_shapes=[pltpu.VMEM((bn, k) if nt else (k, bn), BF16)],
        compiler_params=_params("parallel", "arbitrary"), name="matmul_w32",
    )(a, w)


def _rms_rows(a, g):
    return a * lax.rsqrt(jnp.mean(a * a, axis=-1, keepdims=True) + RMS_EPS) * g


def _rms_mm_kernel(a_ref, g_ref, b_ref, o_ref):
    an = _rms_rows(a_ref[...], g_ref[...]).astype(BF16)
    o_ref[...] = jnp.dot(an, b_ref[...], preferred_element_type=F32).astype(o_ref.dtype)


def _rms_matmul(a, g, b, out_dtype, bm, bn):
    m, k = a.shape
    n = b.shape[1]
    bm = min(bm, m)
    return pl.pallas_call(
        _rms_mm_kernel, grid=(n // bn, m // bm),
        in_specs=[pl.BlockSpec((bm, k), lambda j, i: (i, 0)),
                  pl.BlockSpec((1, k), lambda j, i: (0, 0)),
                  pl.BlockSpec((k, bn), lambda j, i: (0, j))],
        out_specs=pl.BlockSpec((bm, bn), lambda j, i: (i, j)),
        out_shape=jax.ShapeDtypeStruct((m, n), out_dtype),
        compiler_params=_params("parallel", "parallel"), name="rms_matmul",
    )(a, g, b)


def _gdn_gate_kernel(ab_ref, alog_ref, dtb_ref, beta_ref, gc_ref, *, chunk):
    b_raw = ab_ref[0:HEADS, :]
    a_raw = ab_ref[HEADS:2 * HEADS, :]
    beta_ref[...] = 1.0 / (1.0 + jnp.exp(-b_raw))
    xx = a_raw + dtb_ref[...]
    softplus = jnp.maximum(xx, 0.0) + jnp.log(1.0 + jnp.exp(-jnp.abs(xx)))
    g = -jnp.exp(alog_ref[...]) * softplus
    pos = lax.broadcasted_iota(jnp.int32, g.shape, 1) % chunk
    sh = 1
    while sh < chunk:
        g = g + jnp.where(pos >= sh, pltpu.roll(g, sh, axis=1), 0.0)
        sh *= 2
    gc_ref[...] = g


def _gdn_gates(ab_t, a_log, dt_bias, chunk, tn=2048):
    two_h, s = ab_t.shape
    tn = min(tn, s)
    col = pl.BlockSpec((HEADS, 1), lambda j: (0, 0))
    out = pl.BlockSpec((HEADS, tn), lambda j: (0, j))
    return pl.pallas_call(
        functools.partial(_gdn_gate_kernel, chunk=chunk), grid=(s // tn,),
        in_specs=[pl.BlockSpec((two_h, tn), lambda j: (0, j)), col, col],
        out_specs=[out, out],
        out_shape=[jax.ShapeDtypeStruct((HEADS, s), F32)] * 2,
        compiler_params=_params("parallel"), name="gdn_gates",
    )(ab_t, a_log.reshape(HEADS, 1), dt_bias.reshape(HEADS, 1))


def _bmm(a, b):
    return jnp.einsum("nik,nkj->nij", a.astype(BF16), b.astype(BF16), preferred_element_type=F32)


def _unit_lower_inverse(a_mat, row, col, eye, base):
    size = a_mat.shape[-1]
    same = lambda b: (row // b) == (col // b)
    a0 = jnp.where(same(base), a_mat, 0.0)
    p = eye - a0
    m = _bmm(a0, a0)
    n = 1
    while 2 * n < base // 2:
        pm = _bmm(jnp.concatenate([p, m], axis=1), m)
        p = p + pm[:, :size]
        m = pm[:, size:]
        n += n
    p = p + _bmm(p, m)
    b = base
    while b < size:
        a_off = jnp.where(same(2 * b) & jnp.logical_not(same(b)), a_mat, 0.0)
        p = p - _bmm(_bmm(p, a_off), p)
        b *= 2
    return p


def _gdn_core_kernel(q_ref, k_ref, v_ref, z_ref, gc_ref, beta_ref, wq_ref, wk_ref, wv_ref, ng_ref,
                     o_ref, state_ref, qbuf, kbuf, vbuf, u_s, wq_s, qk_s, kdt_s, gl_s, *, chunk):
    bt = q_ref.shape[0]
    dk = HEAD_DIM
    nh = q_ref.shape[1] // dk
    nc = bt // chunk
    halo = CONV_HALO

    @pl.when(pl.program_id(1) == 0)
    def _():
        state_ref[...] = jnp.zeros_like(state_ref)
        for buf in (qbuf, kbuf, vbuf):
            buf[0:halo, :] = jnp.zeros((halo, nh * dk), F32)

    qbuf[halo:halo + bt, :] = q_ref[...].astype(F32)
    kbuf[halo:halo + bt, :] = k_ref[...].astype(F32)
    vbuf[halo:halo + bt, :] = v_ref[...].astype(F32)

    row = lax.broadcasted_iota(jnp.int32, (chunk, chunk), 0)
    col = lax.broadcasted_iota(jnp.int32, (chunk, chunk), 1)
    incl = row >= col
    strict = row > col
    eye = (row == col).astype(F32)

    def conv_silu(buf, w_ref, lanes):
        w = w_ref[:, lanes]
        acc = buf[halo:halo + bt, lanes] * w[GDN_CONV - 1:GDN_CONV, :]
        for j in range(GDN_CONV - 1):
            sh = GDN_CONV - 1 - j
            acc = acc + buf[halo - sh:halo - sh + bt, lanes] * w[j:j + 1, :]
        return _silu(acc)

    def l2n(x):
        return x * lax.rsqrt(jnp.sum(x * x, axis=-1, keepdims=True) + RMS_EPS)

    def chunk_rows(ref, hh):
        tiles = [jnp.broadcast_to(ref[hh:hh + 1, c * chunk:(c + 1) * chunk], (chunk, chunk)) for c in range(nc)]
        return jnp.stack(tiles), jnp.stack([t.T for t in tiles])

    def phase_a(hh):
        lanes = slice(hh * dk, (hh + 1) * dk)
        q = (l2n(conv_silu(qbuf, wq_ref, lanes)) * (dk ** -0.5)).reshape(nc, chunk, dk)
        k = l2n(conv_silu(kbuf, wk_ref, lanes)).reshape(nc, chunk, dk)
        v = conv_silu(vbuf, wv_ref, lanes).reshape(nc, chunk, dk)

        gc_rb, gc_cb = chunk_rows(gc_ref, hh)
        _, bt_cb = chunk_rows(beta_ref, hh)
        decay = jnp.where(incl, jnp.exp(jnp.where(incl, gc_cb - gc_rb, 0.0)), 0.0)

        qkk = jnp.einsum("nid,njd->nij", jnp.concatenate([q, k], axis=1).astype(BF16), k.astype(BF16),
                         preferred_element_type=F32)
        qk = qkk[:, :chunk] * decay
        a_mat = jnp.where(strict, qkk[:, chunk:] * bt_cb * decay, 0.0)
        t_mat = _unit_lower_inverse(a_mat, row, col, eye, GDN_INV_BASE)

        e_gc = jnp.exp(gc_cb)
        uw = _bmm(t_mat, jnp.concatenate([v * bt_cb, k * (bt_cb * e_gc)], axis=2))
        gl_row = gc_cb[:, chunk - 1:chunk, :]
        k_dec = k * jnp.exp(gl_row - gc_cb)

        u_s[:, hh] = uw[:, :, :dk]
        wq_s[:, hh] = jnp.concatenate([uw[:, :, dk:], q * e_gc], axis=1).astype(BF16)
        qk_s[:, hh] = qk.astype(BF16)
        kdt_s[:, hh] = jnp.stack([k_dec[c].T for c in range(nc)]).astype(BF16)
        gl_s[:, hh] = jnp.broadcast_to(jnp.exp(gl_row), (nc, 8, dk))

    for hh in range(nh):
        phase_a(hh)

    def phase_b(c, state):
        ws = _bmm(wq_s[c], state)
        v_new = u_s[c] - ws[:, :chunk]
        vb = v_new.astype(BF16)
        o = ws[:, chunk:] + _bmm(qk_s[c], vb)
        state = state * gl_s[c][:, 0:1, :] + _bmm(kdt_s[c], vb)
        on = o * lax.rsqrt(jnp.mean(o * o, axis=-1, keepdims=True) + RMS_EPS) * ng_ref[...]
        r0 = pl.multiple_of(c * chunk, chunk)
        z = z_ref[pl.ds(r0, chunk), :].astype(F32)
        for hh in range(nh):
            lanes = slice(hh * dk, (hh + 1) * dk)
            o_ref[pl.ds(r0, chunk), lanes] = (on[hh] * _silu(z[:, lanes])).astype(o_ref.dtype)
        return state

    state_ref[...] = lax.fori_loop(0, nc, phase_b, state_ref[...])
    for buf in (qbuf, kbuf, vbuf):
        buf[0:halo, :] = buf[bt:bt + halo, :]


def _gdn_core(proj, gc, beta, w_conv, norm_g, chunk=GDN_CHUNK, bt=GDN_BLOCK, nh=GDN_HEADS_PER_STEP):
    s = proj.shape[0]
    bt = min(bt, s)
    dk = HEAD_DIM
    nc = bt // chunk
    groups = HEADS // nh
    tok = lambda part: pl.BlockSpec((bt, nh * dk), lambda h, b: (b, part * groups + h))
    gate = pl.BlockSpec((None, nh, bt), lambda h, b: (h, 0, b))
    cw = lambda part: pl.BlockSpec((GDN_CONV, nh * dk), lambda h, b: (0, part * groups + h))
    return pl.pallas_call(
        functools.partial(_gdn_core_kernel, chunk=chunk),
        grid=(groups, s // bt),
        in_specs=[tok(0), tok(1), tok(2), tok(3), gate, gate,
                  cw(0), cw(1), cw(2), pl.BlockSpec((1, dk), lambda h, b: (0, 0))],
        out_specs=pl.BlockSpec((bt, nh * dk), lambda h, b: (b, h)),
        out_shape=jax.ShapeDtypeStruct((s, HEADS * dk), BF16),
        scratch_shapes=[pltpu.VMEM((nh, dk, dk), F32)] + [pltpu.VMEM((bt + 2 * CONV_HALO, nh * dk), F32)] * 3
        + [pltpu.VMEM((nc, nh, chunk, dk), F32), pltpu.VMEM((nc, nh, 2 * chunk, dk), BF16),
           pltpu.VMEM((nc, nh, chunk, chunk), BF16), pltpu.VMEM((nc, nh, dk, chunk), BF16),
           pltpu.VMEM((nc, nh, 8, dk), F32)],
        compiler_params=_params("parallel", "arbitrary"), name="gdn_core",
    )(proj, proj, proj, proj, gc.reshape(groups, nh, s), beta.reshape(groups, nh, s),
      w_conv, w_conv, w_conv, norm_g.reshape(1, dk))


def _rope_tab_kernel(pos_ref, invf_ref, cm_ref, sn_ref, sp_ref):
    ang = pos_ref[...].astype(F32) * invf_ref[...]
    cos = jnp.cos(ang)
    sin = jnp.sin(ang)
    lane = lax.broadcasted_iota(jnp.int32, ang.shape, 1)
    half = MLA_ROPE // 2
    cm_ref[...] = jnp.where(lane < MLA_ROPE, cos, 0.0)
    sn_ref[...] = jnp.where(lane < half, -sin, 0.0)
    sp_ref[...] = jnp.where((lane >= half) & (lane < MLA_ROPE), sin, 0.0)


def _rope_tables(positions, bm=512):
    s = positions.shape[0]
    bm = min(bm, s)
    half = MLA_ROPE // 2
    inv_freq = ROPE_THETA ** (-jnp.arange(0, half, dtype=F32) / half)
    invf = jnp.concatenate([inv_freq, inv_freq, jnp.zeros((HEAD_DIM - MLA_ROPE,), F32)]).reshape(1, HEAD_DIM)
    tab = pl.BlockSpec((bm, HEAD_DIM), lambda i: (i, 0))
    return pl.pallas_call(
        _rope_tab_kernel, grid=(s // bm,),
        in_specs=[pl.BlockSpec((bm, 1), lambda i: (i, 0)), pl.BlockSpec((1, HEAD_DIM), lambda i: (0, 0))],
        out_specs=[tab, tab, tab],
        out_shape=[jax.ShapeDtypeStruct((s, HEAD_DIM), F32)] * 3,
        compiler_params=_params("parallel"), name="rope_tables",
    )(positions.reshape(s, 1), invf)


def _rope(x, cm, sn, sp):
    half = MLA_ROPE // 2
    return x * cm + pltpu.roll(x, HEAD_DIM - half, axis=1) * sn + pltpu.roll(x, half, axis=1) * sp


def _q_proj_kernel(a_ref, g_ref, w_ref, cm_ref, sn_ref, sp_ref, o_ref):
    an = _rms_rows(a_ref[...], g_ref[...]).astype(BF16)
    acc = jnp.dot(an, w_ref[...], preferred_element_type=F32)
    scale = MLA_QK ** -0.5 * LOG2_E
    cm, sn, sp = cm_ref[...] * scale, sn_ref[...] * scale, sp_ref[...] * scale
    for g in range(o_ref.shape[1] // MLA_QK_PAD):
        lo = g * MLA_QK_PAD
        hi = lo + HEAD_DIM
        o_ref[:, lo:hi] = (acc[:, lo:hi] * scale).astype(o_ref.dtype)
        o_ref[:, hi:hi + HEAD_DIM] = _rope(acc[:, hi:hi + HEAD_DIM], cm, sn, sp).astype(o_ref.dtype)


def _kv_proj_kernel(a_ref, g_ref, wk_ref, wv_ref, kr_ref, cm_ref, sn_ref, sp_ref, k_ref, v_ref):
    an = _rms_rows(a_ref[...], g_ref[...]).astype(BF16)
    v_ref[...] = jnp.dot(an, wv_ref[...], preferred_element_type=F32).astype(v_ref.dtype)
    acc = jnp.dot(an, wk_ref[...], preferred_element_type=F32)
    kr = _rope(kr_ref[...], cm_ref[...], sn_ref[...], sp_ref[...]).astype(k_ref.dtype)
    for g in range(k_ref.shape[1] // MLA_QK_PAD):
        lo = g * MLA_QK_PAD
        k_ref[:, lo:lo + HEAD_DIM] = acc[:, g * HEAD_DIM:(g + 1) * HEAD_DIM].astype(k_ref.dtype)
        k_ref[:, lo + HEAD_DIM:lo + MLA_QK_PAD] = kr


def _q_proj(a, g, w, tables, bm=1024, heads_per_step=8):
    m, k = a.shape
    bm = min(bm, m)
    bn = heads_per_step * MLA_QK_PAD
    tab = pl.BlockSpec((bm, HEAD_DIM), lambda j, i: (i, 0))
    return pl.pallas_call(
        _q_proj_kernel, grid=(HEADS // heads_per_step, m // bm),
        in_specs=[pl.BlockSpec((bm, k), lambda j, i: (i, 0)),
                  pl.BlockSpec((1, k), lambda j, i: (0, 0)),
                  pl.BlockSpec((k, bn), lambda j, i: (0, j))] + [tab] * len(tables),
        out_specs=pl.BlockSpec((bm, bn), lambda j, i: (i, j)),
        out_shape=jax.ShapeDtypeStruct((m, HEADS * MLA_QK_PAD), BF16),
        compiler_params=_params("parallel", "parallel"), name="q_proj",
    )(a, g, w, *tables)


def _kv_proj(a, g, wk, wv, tables, bm=1024, heads_per_step=8):
    m, k = a.shape
    bm = min(bm, m)
    bw = heads_per_step * HEAD_DIM
    tab = pl.BlockSpec((bm, HEAD_DIM), lambda j, i: (i, 0))
    wspec = pl.BlockSpec((k, bw), lambda j, i: (0, j))
    return pl.pallas_call(
        _kv_proj_kernel, grid=(HEADS // heads_per_step, m // bm),
        in_specs=[pl.BlockSpec((bm, k), lambda j, i: (i, 0)),
                  pl.BlockSpec((1, k), lambda j, i: (0, 0)), wspec, wspec] + [tab] * len(tables),
        out_specs=[pl.BlockSpec((bm, heads_per_step * MLA_QK_PAD), lambda j, i: (i, j)),
                   pl.BlockSpec((bm, bw), lambda j, i: (i, j))],
        out_shape=[jax.ShapeDtypeStruct((m, HEADS * MLA_QK_PAD), BF16),
                   jax.ShapeDtypeStruct((m, HEADS * HEAD_DIM), BF16)],
        compiler_params=_params("parallel", "parallel"), name="kv_proj",
    )(a, g, wk, wv, *tables)


def _latent_kernel(a_ref, w_ref, cq_ref, ckv_ref, kr_ref):
    acc = _dot(a_ref[...], w_ref[...], True)
    cq_ref[...] = acc[:, :MLA_Q_RANK]
    ckv_ref[...] = acc[:, MLA_Q_RANK:MLA_Q_RANK + MLA_KV_RANK]
    kr_ref[...] = acc[:, MLA_Q_RANK + MLA_KV_RANK:]


def _latent_proj(h, w_lat, bm=512):
    m, k = h.shape
    n = w_lat.shape[0]
    bm = min(bm, m)
    widths = (MLA_Q_RANK, MLA_KV_RANK, n - MLA_Q_RANK - MLA_KV_RANK)
    return pl.pallas_call(
        _latent_kernel, grid=(m // bm,),
        in_specs=[pl.BlockSpec((bm, k), lambda i: (i, 0)), pl.BlockSpec((n, k), lambda i: (0, 0))],
        out_specs=[pl.BlockSpec((bm, wd), lambda i: (i, 0)) for wd in widths],
        out_shape=[jax.ShapeDtypeStruct((m, wd), F32) for wd in widths],
        compiler_params=_params("parallel"), name="latent_proj",
    )(h, w_lat)


def _attn_kernel(q_ref, k_ref, v_ref, z_ref, o_ref, sa_ref, sb_ref, m_ref, acc_ref, *, tk):
    i = pl.program_id(1)
    tq = q_ref.shape[0]
    dv = v_ref.shape[1]
    rows = tq // ATT_GROUPS

    def scores(s_ref, blk):
        k_blk = k_ref[pl.ds(pl.multiple_of(blk * tk, tk), tk), :]
        s_ref[...] = lax.dot_general(q_ref[...], k_blk, (((1,), (1,)), ((), ())), preferred_element_type=F32)

    def softmax_pv(s_ref, blk, masked):
        start = pl.multiple_of(blk * tk, tk)
        for g in range(ATT_GROUPS):
            width = (g + 1) * rows if masked else tk
            tiles = width // dv
            s = s_ref[g * rows:(g + 1) * rows, 0:width]
            if masked:
                row = lax.broadcasted_iota(jnp.int32, (rows, width), 0) + g * rows
                col = lax.broadcasted_iota(jnp.int32, (rows, width), 1)
                s = jnp.where(col <= row, s, -jnp.inf)
            m_part = s[:, :dv]
            for t in range(1, tiles):
                m_part = jnp.maximum(m_part, s[:, t * dv:(t + 1) * dv])
            m_prev = m_ref[g]
            m_new = jnp.maximum(m_prev, jnp.broadcast_to(jnp.max(m_part, axis=-1, keepdims=True), (rows, dv)))
            alpha = jnp.exp2(m_prev - m_new)
            p = jnp.exp2(s - jnp.concatenate([m_new] * tiles, axis=1)).astype(BF16)
            v_ext = jnp.concatenate([v_ref[pl.ds(start, width), :], jnp.ones((width, dv), BF16)], axis=1)
            acc_ref[g] = (jnp.concatenate([alpha, alpha], axis=1) * acc_ref[g]
                          + jnp.dot(p, v_ext, preferred_element_type=F32))
            m_ref[g] = m_new

    def finish():
        for g in range(ATT_GROUPS):
            acc = acc_ref[g]
            o = acc[:, :dv] / acc[:, dv:]
            z = z_ref[g * rows:(g + 1) * rows, :].astype(F32)
            o_ref[g * rows:(g + 1) * rows, :] = (o * _silu(z)).astype(o_ref.dtype)

    m_ref[...] = jnp.full(m_ref.shape, -jnp.inf, F32)
    acc_ref[...] = jnp.zeros(acc_ref.shape, F32)
    scores(sa_ref, 0)

    def pair(jj, carry):
        blk = 2 * jj
        scores(sb_ref, blk + 1)
        softmax_pv(sa_ref, blk, False)
        scores(sa_ref, blk + 2)
        softmax_pv(sb_ref, blk + 1, False)
        return carry

    lax.fori_loop(0, i // 2, pair, 0)

    @pl.when(i % 2 == 0)
    def _():
        softmax_pv(sa_ref, i, True)
        finish()

    @pl.when(i % 2 == 1)
    def _():
        scores(sb_ref, i)
        softmax_pv(sa_ref, i - 1, False)
        softmax_pv(sb_ref, i, True)
        finish()


def _attention(q, k, v, z, tq=ATT_TQ):
    s = q.shape[0]
    tq = min(tq, s)
    tk = tq
    rows = tq // ATT_GROUPS
    return pl.pallas_call(
        functools.partial(_attn_kernel, tk=tk), grid=(HEADS, s // tq),
        in_specs=[pl.BlockSpec((tq, MLA_QK_PAD), lambda h, i: (i, h)),
                  pl.BlockSpec((s, MLA_QK_PAD), lambda h, i: (0, h)),
                  pl.BlockSpec((s, HEAD_DIM), lambda h, i: (0, h)),
                  pl.BlockSpec((tq, HEAD_DIM), lambda h, i: (i, h))],
        out_specs=pl.BlockSpec((tq, HEAD_DIM), lambda h, i: (i, h)),
        out_shape=jax.ShapeDtypeStruct((s, HEADS * HEAD_DIM), BF16),
        scratch_shapes=[pltpu.VMEM((tq, tk), F32), pltpu.VMEM((tq, tk), F32),
                        pltpu.VMEM((ATT_GROUPS, rows, HEAD_DIM), F32),
                        pltpu.VMEM((ATT_GROUPS, rows, 2 * HEAD_DIM), F32)],
        compiler_params=_params("parallel", "arbitrary"), name="mla_attention",
    )(q, k, v, z)


def _gdn_branch(h, w_in, w_conv, a_log, dt_bias, norm_g, w_out):
    width = 4 * HEADS * HEAD_DIM
    w_in_t = w_in.T
    proj = _matmul_w32(h, w_in_t, width, BF16, bm=512, bn=1024, nt=True)
    w_gate = jnp.pad(w_in_t[width:], ((0, 128 - 2 * HEADS), (0, 0))).astype(BF16)
    ab = _matmul(h, w_gate, F32, bm=1024, bn=128, nt=True)[:, :2 * HEADS]
    beta, gc = _gdn_gates(ab.T, a_log, dt_bias, GDN_CHUNK)
    gated = _gdn_core(proj, gc, beta, w_conv, norm_g)
    return _matmul_w32(gated, w_out, w_out.shape[1], BF16, bm=1024, bn=512)


def _mla_branch(h, positions, w_in, q_norm_g, w_qb, kv_norm_g, w_kvb, w_out):
    s = h.shape[0]
    n_lat = MLA_Q_RANK + MLA_KV_RANK + MLA_ROPE
    w_in_t = w_in.T
    w_lat = jnp.pad(w_in_t[:n_lat], ((0, HEAD_DIM - MLA_ROPE), (0, 0))).astype(BF16)
    cq, ckv, k_rope = _latent_proj(h, w_lat)
    z = _matmul(h, w_in_t[n_lat:].astype(BF16), BF16, bm=1024, bn=1024, nt=True)

    wq = w_qb.reshape(MLA_Q_RANK, HEADS, MLA_QK)
    wq = jnp.pad(wq, ((0, 0), (0, 0), (0, MLA_QK_PAD - MLA_QK))).reshape(MLA_Q_RANK, HEADS * MLA_QK_PAD)
    wkv = w_kvb.reshape(MLA_KV_RANK, HEADS, 2 * HEAD_DIM)
    wk = wkv[:, :, :HEAD_DIM].reshape(MLA_KV_RANK, HEADS * HEAD_DIM)
    wv = wkv[:, :, HEAD_DIM:].reshape(MLA_KV_RANK, HEADS * HEAD_DIM)

    cm, sn, sp = _rope_tables(positions.reshape(s))
    q = _q_proj(cq, q_norm_g.reshape(1, -1), wq.astype(BF16), (cm, sn, sp))
    k, v = _kv_proj(ckv, kv_norm_g.reshape(1, -1), wk.astype(BF16), wv.astype(BF16), (k_rope, cm, sn, sp))
    gated = _attention(q, k, v, z)
    return _matmul_w32(gated, w_out, w_out.shape[1], BF16, bm=1024, bn=512)


def kernel(x, c, positions, w_mod, b_mod, ln_g, ln_b, a_w_in, a_w_conv, a_a_log, a_dt_bias, a_norm_g, a_w_out,
           b_w_in, b_q_norm_g, b_w_qb, b_kv_norm_g, b_w_kvb, b_w_out):
    bsz, s, d = x.shape
    assert bsz == 1 and d == D_MODEL and DEPTH == 2
    x0 = x.reshape(s, d)
    mod = _modulation(c, w_mod, b_mod)
    vec = lambda t: t.reshape(1, d)

    h0 = _modulate(x0, mod[0, 1], mod[0, 0])
    y0 = _gdn_branch(h0, a_w_in[0], a_w_conv[0], a_a_log[0], a_dt_bias[0], a_norm_g[0], a_w_out[0])
    x1, h1 = _residual_layernorm(x0, y0, mod[0, 2], vec(ln_g[0]), vec(ln_b[0]), mod[1, 1], mod[1, 0])
    y1 = _mla_branch(h1, positions, b_w_in[0], b_q_norm_g[0], b_w_qb[0], b_kv_norm_g[0], b_w_kvb[0], b_w_out[0])
    x2 = _residual_layernorm(x1, y1, mod[1, 2], vec(ln_g[1]), vec(ln_b[1]))
    return x2.reshape(bsz, s, d)
```

```python
import functools

import jax
import jax.numpy as jnp
from jax import lax
from jax.experimental import pallas as pl
from jax.experimental.pallas import tpu as pltpu

F32 = jnp.float32
BF16 = jnp.bfloat16

D_MODEL = 4096
DEPTH = 2
DEEPNORM_ALPHA = (2.0 * DEPTH) ** 0.25
RMS_EPS = 1e-6
LN_EPS = 1e-5

HEADS = 32
HEAD_DIM = 128
GDN_CONV = 4
GDN_CHUNK = 128
GDN_BLOCK = 1024
GDN_HEADS_PER_STEP = 4
GDN_INV_BASE = 32
CONV_HALO = 8

MLA_Q_RANK = 896
MLA_KV_RANK = 512
MLA_ROPE = 64
MLA_QK = HEAD_DIM + MLA_ROPE
MLA_QK_PAD = 256
ROPE_THETA = 10000.0
LOG2_E = 1.4426950408889634
ATT_TQ = 1024
ATT_GROUPS = 4

VMEM_LIMIT = 56 * 1024 * 1024

IN_PROJ_BLOCK = (512, 1024)
OUT_PROJ_BLOCK = (1024, 512)
Z_PROJ_BLOCK = (1024, 1024)
GATE_PROJ_BLOCK = (1024, 128)


def _params(*sem):
    return pltpu.CompilerParams(dimension_semantics=sem, vmem_limit_bytes=VMEM_LIMIT)


def _silu(x):
    return x / (1.0 + jnp.exp(-x))


def _mod_kernel(c_ref, w_ref, b_ref, o_ref):
    tn = o_ref.shape[-1]
    rows = 128

    def body(kk, acc):
        r = pl.multiple_of(kk * rows, rows)
        c = c_ref[pl.ds(r, rows), :]
        a = _silu(c)
        p = w_ref[pl.ds(r, rows), :] * jnp.concatenate([a] * (tn // 128), axis=1)
        for t in range(rows // 8):
            acc = acc + p[t * 8:(t + 1) * 8, :]
        return acc

    acc = lax.fori_loop(0, c_ref.shape[0] // rows, body, jnp.zeros((8, tn), F32))
    o_ref[...] = jnp.sum(acc, axis=0, keepdims=True) + b_ref[...]


def _modulation(c, w_mod, b_mod, tn=512):
    depth, d, n = w_mod.shape
    c_rep = jnp.broadcast_to(c.reshape(d, 1), (d, 128))
    out = pl.pallas_call(
        _mod_kernel,
        grid=(depth, n // tn),
        in_specs=[pl.BlockSpec((d, 128), lambda l, j: (0, 0)),
                  pl.BlockSpec((None, d, tn), lambda l, j: (l, 0, j)),
                  pl.BlockSpec((None, 1, tn), lambda l, j: (l, 0, j))],
        out_specs=pl.BlockSpec((None, 1, tn), lambda l, j: (l, 0, j)),
        out_shape=jax.ShapeDtypeStruct((depth, 1, n), F32),
        compiler_params=_params("parallel", "parallel"),
        name="adaln_mod",
    )(c_rep, w_mod, b_mod.reshape(depth, 1, n))
    return out.reshape(depth, 3, 1, d)


def _modulate_kernel(x_ref, sc_ref, sh_ref, h_ref):
    h_ref[...] = (x_ref[...] * (1.0 + sc_ref[...]) + sh_ref[...]).astype(h_ref.dtype)


def _modulate(x, scale, shift, bm=512):
    s, d = x.shape
    row = pl.BlockSpec((bm, d), lambda i: (i, 0))
    vec = pl.BlockSpec((1, d), lambda i: (0, 0))
    return pl.pallas_call(
        _modulate_kernel, grid=(s // bm,), in_specs=[row, vec, vec], out_specs=row,
        out_shape=jax.ShapeDtypeStruct((s, d), BF16),
        compiler_params=_params("parallel"), name="modulate",
    )(x, scale, shift)


def _resid_ln(x_ref, y_ref, gate_ref, g_ref, b_ref):
    t = DEEPNORM_ALPHA * x_ref[...] + (1.0 + gate_ref[...]) * y_ref[...].astype(F32)
    mu = jnp.mean(t, axis=-1, keepdims=True)
    dlt = t - mu
    var = jnp.mean(dlt * dlt, axis=-1, keepdims=True)
    return dlt * lax.rsqrt(var + LN_EPS) * g_ref[...] + b_ref[...]


def _resid_ln_kernel(x_ref, y_ref, gate_ref, g_ref, b_ref, xo_ref):
    xo_ref[...] = _resid_ln(x_ref, y_ref, gate_ref, g_ref, b_ref)


def _resid_ln_mod_kernel(x_ref, y_ref, gate_ref, g_ref, b_ref, sc_ref, sh_ref, xo_ref, h_ref):
    xn = _resid_ln(x_ref, y_ref, gate_ref, g_ref, b_ref)
    xo_ref[...] = xn
    h_ref[...] = (xn * (1.0 + sc_ref[...]) + sh_ref[...]).astype(h_ref.dtype)


def _residual_layernorm(x, y, gate, ln_g, ln_b, next_scale=None, next_shift=None, bm=256):
    s, d = x.shape
    row = pl.BlockSpec((bm, d), lambda i: (i, 0))
    vec = pl.BlockSpec((1, d), lambda i: (0, 0))
    if next_scale is None:
        return pl.pallas_call(
            _resid_ln_kernel, grid=(s // bm,), in_specs=[row, row, vec, vec, vec], out_specs=row,
            out_shape=jax.ShapeDtypeStruct((s, d), F32),
            compiler_params=_params("parallel"), name="resid_ln",
        )(x, y, gate, ln_g, ln_b)
    return pl.pallas_call(
        _resid_ln_mod_kernel, grid=(s // bm,), in_specs=[row, row, vec, vec, vec, vec, vec],
        out_specs=[row, row],
        out_shape=[jax.ShapeDtypeStruct((s, d), F32), jax.ShapeDtypeStruct((s, d), BF16)],
        compiler_params=_params("parallel"), name="resid_ln_mod",
    )(x, y, gate, ln_g, ln_b, next_scale, next_shift)


_NT_DIMS = (((1,), (1,)), ((), ()))


def _dot(a, b, nt):
    if nt:
        return lax.dot_general(a, b, _NT_DIMS, preferred_element_type=F32)
    return jnp.dot(a, b, preferred_element_type=F32)


def _weight_spec(k, bn, nt):
    if nt:
        return pl.BlockSpec((bn, k), lambda j, i: (j, 0))
    return pl.BlockSpec((k, bn), lambda j, i: (0, j))


def _mm_kernel(a_ref, b_ref, o_ref, *, nt):
    o_ref[...] = _dot(a_ref[...], b_ref[...], nt).astype(o_ref.dtype)


def _matmul(a, b, out_dtype, bm, bn, nt=False):
    m, k = a.shape
    n = b.shape[0] if nt else b.shape[1]
    bm = min(bm, m)
    return pl.pallas_call(
        functools.partial(_mm_kernel, nt=nt), grid=(n // bn, m // bm),
        in_specs=[pl.BlockSpec((bm, k), lambda j, i: (i, 0)), _weight_spec(k, bn, nt)],
        out_specs=pl.BlockSpec((bm, bn), lambda j, i: (i, j)),
        out_shape=jax.ShapeDtypeStruct((m, n), out_dtype),
        compiler_params=_params("parallel", "parallel"), name="matmul",
    )(a, b)


def _mm_w32_kernel(a_ref, w_ref, o_ref, wb_ref, *, nt):
    @pl.when(pl.program_id(1) == 0)
    def _():
        wb_ref[...] = w_ref[...].astype(BF16)

    o_ref[...] = _dot(a_ref[...], wb_ref[...], nt).astype(o_ref.dtype)


def _matmul_w32(a, w, n, out_dtype, bm, bn, nt=False):
    m, k = a.shape
    bm = min(bm, m)
    return pl.pallas_call(
        functools.partial(_mm_w32_kernel, nt=nt), grid=(n // bn, m // bm),
        in_specs=[pl.BlockSpec((bm, k), lambda j, i: (i, 0)), _weight_spec(k, bn, nt)],
        out_specs=pl.BlockSpec((bm, bn), lambda j, i: (i, j)),
        out_shape=jax.ShapeDtypeStruct((m, n), out_dtype),
        scratch_shapes=[pltpu.VMEM((bn, k) if nt else (k, bn), BF16)],
        compiler_params=_params("parallel", "arbitrary"), name="matmul_w32",
    )(a, w)


def _rms_rows(a, g):
    return a * lax.rsqrt(jnp.mean(a * a, axis=-1, keepdims=True) + RMS_EPS) * g


def _gdn_gate_kernel(ab_ref, alog_ref, dtb_ref, beta_ref, gc_ref, *, chunk):
    b_raw = ab_ref[0:HEADS, :]
    a_raw = ab_ref[HEADS:2 * HEADS, :]
    beta_ref[...] = 1.0 / (1.0 + jnp.exp(-b_raw))
    xx = a_raw + dtb_ref[...]
    softplus = jnp.maximum(xx, 0.0) + jnp.log(1.0 + jnp.exp(-jnp.abs(xx)))
    g = -jnp.exp(alog_ref[...]) * softplus
    pos = lax.broadcasted_iota(jnp.int32, g.shape, 1) % chunk
    sh = 1
    while sh < chunk:
        g = g + jnp.where(pos >= sh, pltpu.roll(g, sh, axis=1), 0.0)
        sh *= 2
    gc_ref[...] = g


def _gdn_gates(ab_t, a_log, dt_bias, chunk, tn=2048):
    two_h, s = ab_t.shape
    tn = min(tn, s)
    col = pl.BlockSpec((HEADS, 1), lambda j: (0, 0))
    out = pl.BlockSpec((HEADS, tn), lambda j: (0, j))
    return pl.pallas_call(
        functools.partial(_gdn_gate_kernel, chunk=chunk), grid=(s // tn,),
        in_specs=[pl.BlockSpec((two_h, tn), lambda j: (0, j)), col, col],
        out_specs=[out, out],
        out_shape=[jax.ShapeDtypeStruct((HEADS, s), F32)] * 2,
        compiler_params=_params("parallel"), name="gdn_gates",
    )(ab_t, a_log.reshape(HEADS, 1), dt_bias.reshape(HEADS, 1))


def _bmm(a, b):
    return jnp.einsum("nik,nkj->nij", a.astype(BF16), b.astype(BF16), preferred_element_type=F32)


def _unit_lower_inverse(a_mat, row, col, eye, base):
    size = a_mat.shape[-1]
    same = lambda b: (row // b) == (col // b)
    a0 = jnp.where(same(base), a_mat, 0.0)
    p = eye - a0
    m = _bmm(a0, a0)
    n = 1
    while 2 * n < base // 2:
        pm = _bmm(jnp.concatenate([p, m], axis=1), m)
        p = p + pm[:, :size]
        m = pm[:, size:]
        n += n
    p = p + _bmm(p, m)
    b = base
    while b < size:
        a_off = jnp.where(same(2 * b) & jnp.logical_not(same(b)), a_mat, 0.0)
        p = p - _bmm(_bmm(p, a_off), p)
        b *= 2
    return p


def _gdn_core_kernel(q_ref, k_ref, v_ref, z_ref, gc_ref, beta_ref, wq_ref, wk_ref, wv_ref, ng_ref,
                     o_ref, state_ref, qbuf, kbuf, vbuf, u_s, wq_s, qk_s, kdt_s, gl_s, *, chunk):
    bt = q_ref.shape[0]
    dk = HEAD_DIM
    nh = q_ref.shape[1] // dk
    nc = bt // chunk
    halo = CONV_HALO

    @pl.when(pl.program_id(1) == 0)
    def _():
        state_ref[...] = jnp.zeros_like(state_ref)
        for buf in (qbuf, kbuf, vbuf):
            buf[0:halo, :] = jnp.zeros((halo, nh * dk), F32)

    qbuf[halo:halo + bt, :] = q_ref[...].astype(F32)
    kbuf[halo:halo + bt, :] = k_ref[...].astype(F32)
    vbuf[halo:halo + bt, :] = v_ref[...].astype(F32)

    row = lax.broadcasted_iota(jnp.int32, (chunk, chunk), 0)
    col = lax.broadcasted_iota(jnp.int32, (chunk, chunk), 1)
    incl = row >= col
    strict = row > col
    eye = (row == col).astype(F32)

    def conv_silu(buf, w_ref, lanes):
        w = w_ref[:, lanes]
        acc = buf[halo:halo + bt, lanes] * w[GDN_CONV - 1:GDN_CONV, :]
        for j in range(GDN_CONV - 1):
            sh = GDN_CONV - 1 - j
            acc = acc + buf[halo - sh:halo - sh + bt, lanes] * w[j:j + 1, :]
        return _silu(acc)

    def l2n(x, scale=1.0):
        return x * (lax.rsqrt(jnp.sum(x * x, axis=-1, keepdims=True) + RMS_EPS) * scale)

    def chunk_rows(ref, hh):
        tiles = [jnp.broadcast_to(ref[hh:hh + 1, c * chunk:(c + 1) * chunk], (chunk, chunk)) for c in range(nc)]
        return jnp.stack(tiles), jnp.stack([t.T for t in tiles])

    def phase_a(hh):
        lanes = slice(hh * dk, (hh + 1) * dk)
        q = l2n(conv_silu(qbuf, wq_ref, lanes), dk ** -0.5).reshape(nc, chunk, dk)
        k = l2n(conv_silu(kbuf, wk_ref, lanes)).reshape(nc, chunk, dk)
        v = conv_silu(vbuf, wv_ref, lanes).reshape(nc, chunk, dk)

        gc_rb, gc_cb = chunk_rows(gc_ref, hh)
        _, bt_cb = chunk_rows(beta_ref, hh)
        decay = jnp.exp(jnp.where(incl, gc_cb - gc_rb, -jnp.inf))

        qkk = jnp.einsum("nid,njd->nij", jnp.concatenate([q, k], axis=1).astype(BF16), k.astype(BF16),
                         preferred_element_type=F32)
        qk = qkk[:, :chunk] * decay
        a_mat = jnp.where(strict, qkk[:, chunk:] * bt_cb * decay, 0.0)
        t_mat = _unit_lower_inverse(a_mat, row, col, eye, GDN_INV_BASE)

        e_gc = jnp.exp(gc_cb)
        uw = _bmm(t_mat, jnp.concatenate([v * bt_cb, k * (bt_cb * e_gc)], axis=2))
        gl_row = gc_cb[:, chunk - 1:chunk, :]
        k_dec = k * jnp.exp(gl_row - gc_cb)

        u_s[:, hh] = uw[:, :, :dk]
        wq_s[:, hh] = jnp.concatenate([uw[:, :, dk:], q * e_gc], axis=1).astype(BF16)
        qk_s[:, hh] = qk.astype(BF16)
        kdt_s[:, hh] = jnp.stack([k_dec[c].T for c in range(nc)]).astype(BF16)
        gl_s[:, hh] = jnp.broadcast_to(jnp.exp(gl_row), (nc, 8, dk))

    for hh in range(nh):
        phase_a(hh)

    def phase_b(c, state):
        ws = _bmm(wq_s[c], state)
        v_new = u_s[c] - ws[:, :chunk]
        vb = v_new.astype(BF16)
        o = ws[:, chunk:] + _bmm(qk_s[c], vb)
        state = state * gl_s[c][:, 0:1, :] + _bmm(kdt_s[c], vb)
        on = o * lax.rsqrt(jnp.mean(o * o, axis=-1, keepdims=True) + RMS_EPS) * ng_ref[...]
        r0 = pl.multiple_of(c * chunk, chunk)
        z = z_ref[pl.ds(r0, chunk), :].astype(F32)
        for hh in range(nh):
            lanes = slice(hh * dk, (hh + 1) * dk)
            o_ref[pl.ds(r0, chunk), lanes] = (on[hh] * _silu(z[:, lanes])).astype(o_ref.dtype)
        return state

    state_ref[...] = lax.fori_loop(0, nc, phase_b, state_ref[...])
    for buf in (qbuf, kbuf, vbuf):
        buf[0:halo, :] = buf[bt:bt + halo, :]


def _gdn_core(proj, gc, beta, w_conv, norm_g, chunk=GDN_CHUNK, bt=GDN_BLOCK, nh=GDN_HEADS_PER_STEP):
    s = proj.shape[0]
    bt = min(bt, s)
    dk = HEAD_DIM
    nc = bt // chunk
    groups = HEADS // nh
    tok = lambda part: pl.BlockSpec((bt, nh * dk), lambda h, b: (b, part * groups + h))
    gate = pl.BlockSpec((None, nh, bt), lambda h, b: (h, 0, b))
    cw = lambda part: pl.BlockSpec((GDN_CONV, nh * dk), lambda h, b: (0, part * groups + h))
    return pl.pallas_call(
        functools.partial(_gdn_core_kernel, chunk=chunk),
        grid=(groups, s // bt),
        in_specs=[tok(0), tok(1), tok(2), tok(3), gate, gate,
                  cw(0), cw(1), cw(2), pl.BlockSpec((1, dk), lambda h, b: (0, 0))],
        out_specs=pl.BlockSpec((bt, nh * dk), lambda h, b: (b, h)),
        out_shape=jax.ShapeDtypeStruct((s, HEADS * dk), BF16),
        scratch_shapes=[pltpu.VMEM((nh, dk, dk), F32)] + [pltpu.VMEM((bt + 2 * CONV_HALO, nh * dk), F32)] * 3
        + [pltpu.VMEM((nc, nh, chunk, dk), F32), pltpu.VMEM((nc, nh, 2 * chunk, dk), BF16),
           pltpu.VMEM((nc, nh, chunk, chunk), BF16), pltpu.VMEM((nc, nh, dk, chunk), BF16),
           pltpu.VMEM((nc, nh, 8, dk), F32)],
        compiler_params=_params("parallel", "arbitrary"), name="gdn_core",
    )(proj, proj, proj, proj, gc.reshape(groups, nh, s), beta.reshape(groups, nh, s),
      w_conv, w_conv, w_conv, norm_g.reshape(1, dk))


def _rope_tab_kernel(pos_ref, invf_ref, cm_ref, sn_ref, sp_ref):
    ang = pos_ref[...].astype(F32) * invf_ref[...]
    cos = jnp.cos(ang)
    sin = jnp.sin(ang)
    lane = lax.broadcasted_iota(jnp.int32, ang.shape, 1)
    half = MLA_ROPE // 2
    cm_ref[...] = jnp.where(lane < MLA_ROPE, cos, 0.0)
    sn_ref[...] = jnp.where(lane < half, -sin, 0.0)
    sp_ref[...] = jnp.where((lane >= half) & (lane < MLA_ROPE), sin, 0.0)


def _rope_tables(positions, bm=512):
    s = positions.shape[0]
    bm = min(bm, s)
    half = MLA_ROPE // 2
    inv_freq = ROPE_THETA ** (-jnp.arange(0, half, dtype=F32) / half)
    invf = jnp.concatenate([inv_freq, inv_freq, jnp.zeros((HEAD_DIM - MLA_ROPE,), F32)]).reshape(1, HEAD_DIM)
    tab = pl.BlockSpec((bm, HEAD_DIM), lambda i: (i, 0))
    return pl.pallas_call(
        _rope_tab_kernel, grid=(s // bm,),
        in_specs=[pl.BlockSpec((bm, 1), lambda i: (i, 0)), pl.BlockSpec((1, HEAD_DIM), lambda i: (0, 0))],
        out_specs=[tab, tab, tab],
        out_shape=[jax.ShapeDtypeStruct((s, HEAD_DIM), F32)] * 3,
        compiler_params=_params("parallel"), name="rope_tables",
    )(positions.reshape(s, 1), invf)


def _rope(x, cm, sn, sp):
    half = MLA_ROPE // 2
    return x * cm + pltpu.roll(x, HEAD_DIM - half, axis=1) * sn + pltpu.roll(x, half, axis=1) * sp


def _q_proj_kernel(a_ref, g_ref, w_ref, cm_ref, sn_ref, sp_ref, o_ref):
    an = _rms_rows(a_ref[...], g_ref[...]).astype(BF16)
    acc = jnp.dot(an, w_ref[...], preferred_element_type=F32)
    scale = MLA_QK ** -0.5 * LOG2_E
    cm, sn, sp = cm_ref[...] * scale, sn_ref[...] * scale, sp_ref[...] * scale
    for g in range(o_ref.shape[1] // MLA_QK_PAD):
        lo = g * MLA_QK_PAD
        hi = lo + HEAD_DIM
        o_ref[:, lo:hi] = (acc[:, lo:hi] * scale).astype(o_ref.dtype)
        o_ref[:, hi:hi + HEAD_DIM] = _rope(acc[:, hi:hi + HEAD_DIM], cm, sn, sp).astype(o_ref.dtype)


def _kv_proj_kernel(a_ref, g_ref, wk_ref, wv_ref, kr_ref, cm_ref, sn_ref, sp_ref, k_ref, v_ref):
    an = _rms_rows(a_ref[...], g_ref[...]).astype(BF16)
    v_ref[...] = jnp.dot(an, wv_ref[...], preferred_element_type=F32).astype(v_ref.dtype)
    acc = jnp.dot(an, wk_ref[...], preferred_element_type=F32)
    kr = _rope(kr_ref[...], cm_ref[...], sn_ref[...], sp_ref[...]).astype(k_ref.dtype)
    for g in range(k_ref.shape[1] // MLA_QK_PAD):
        lo = g * MLA_QK_PAD
        k_ref[:, lo:lo + HEAD_DIM] = acc[:, g * HEAD_DIM:(g + 1) * HEAD_DIM].astype(k_ref.dtype)
        k_ref[:, lo + HEAD_DIM:lo + MLA_QK_PAD] = kr


def _q_proj(a, g, w, tables, bm=1024, heads_per_step=8):
    m, k = a.shape
    bm = min(bm, m)
    bn = heads_per_step * MLA_QK_PAD
    tab = pl.BlockSpec((bm, HEAD_DIM), lambda j, i: (i, 0))
    return pl.pallas_call(
        _q_proj_kernel, grid=(HEADS // heads_per_step, m // bm),
        in_specs=[pl.BlockSpec((bm, k), lambda j, i: (i, 0)),
                  pl.BlockSpec((1, k), lambda j, i: (0, 0)),
                  pl.BlockSpec((k, bn), lambda j, i: (0, j))] + [tab] * len(tables),
        out_specs=pl.BlockSpec((bm, bn), lambda j, i: (i, j)),
        out_shape=jax.ShapeDtypeStruct((m, HEADS * MLA_QK_PAD), BF16),
        compiler_params=_params("parallel", "parallel"), name="q_proj",
    )(a, g, w, *tables)


def _kv_proj(a, g, wk, wv, tables, bm=1024, heads_per_step=8):
    m, k = a.shape
    bm = min(bm, m)
    bw = heads_per_step * HEAD_DIM
    tab = pl.BlockSpec((bm, HEAD_DIM), lambda j, i: (i, 0))
    wspec = pl.BlockSpec((k, bw), lambda j, i: (0, j))
    return pl.pallas_call(
        _kv_proj_kernel, grid=(HEADS // heads_per_step, m // bm),
        in_specs=[pl.BlockSpec((bm, k), lambda j, i: (i, 0)),
                  pl.BlockSpec((1, k), lambda j, i: (0, 0)), wspec, wspec] + [tab] * len(tables),
        out_specs=[pl.BlockSpec((bm, heads_per_step * MLA_QK_PAD), lambda j, i: (i, j)),
                   pl.BlockSpec((bm, bw), lambda j, i: (i, j))],
        out_shape=[jax.ShapeDtypeStruct((m, HEADS * MLA_QK_PAD), BF16),
                   jax.ShapeDtypeStruct((m, HEADS * HEAD_DIM), BF16)],
        compiler_params=_params("parallel", "parallel"), name="kv_proj",
    )(a, g, wk, wv, *tables)


def _latent_kernel(a_ref, w_ref, cq_ref, ckv_ref, kr_ref):
    acc = _dot(a_ref[...], w_ref[...], True)
    cq_ref[...] = acc[:, :MLA_Q_RANK]
    ckv_ref[...] = acc[:, MLA_Q_RANK:MLA_Q_RANK + MLA_KV_RANK]
    kr_ref[...] = acc[:, MLA_Q_RANK + MLA_KV_RANK:]


def _latent_proj(h, w_lat, bm=512):
    m, k = h.shape
    n = w_lat.shape[0]
    bm = min(bm, m)
    widths = (MLA_Q_RANK, MLA_KV_RANK, n - MLA_Q_RANK - MLA_KV_RANK)
    return pl.pallas_call(
        _latent_kernel, grid=(m // bm,),
        in_specs=[pl.BlockSpec((bm, k), lambda i: (i, 0)), pl.BlockSpec((n, k), lambda i: (0, 0))],
        out_specs=[pl.BlockSpec((bm, wd), lambda i: (i, 0)) for wd in widths],
        out_shape=[jax.ShapeDtypeStruct((m, wd), F32) for wd in widths],
        compiler_params=_params("parallel"), name="latent_proj",
    )(h, w_lat)


def _attn_kernel(q_ref, k_ref, v_ref, z_ref, o_ref, sa_ref, sb_ref, m_ref, acc_ref, *, tk):
    i = pl.program_id(1)
    tq = q_ref.shape[0]
    dv = v_ref.shape[1]
    rows = tq // ATT_GROUPS

    def scores(s_ref, blk):
        k_blk = k_ref[pl.ds(pl.multiple_of(blk * tk, tk), tk), :]
        s_ref[...] = lax.dot_general(q_ref[...], k_blk, (((1,), (1,)), ((), ())), preferred_element_type=F32)

    def softmax_pv(s_ref, blk, masked):
        start = pl.multiple_of(blk * tk, tk)
        for g in range(ATT_GROUPS):
            width = (g + 1) * rows if masked else tk
            tiles = width // dv
            s = s_ref[g * rows:(g + 1) * rows, 0:width]
            if masked:
                row = lax.broadcasted_iota(jnp.int32, (rows, width), 0) + g * rows
                col = lax.broadcasted_iota(jnp.int32, (rows, width), 1)
                s = jnp.where(col <= row, s, -jnp.inf)
            m_part = s[:, :dv]
            for t in range(1, tiles):
                m_part = jnp.maximum(m_part, s[:, t * dv:(t + 1) * dv])
            m_prev = m_ref[g]
            m_new = jnp.maximum(m_prev, jnp.broadcast_to(jnp.max(m_part, axis=-1, keepdims=True), (rows, dv)))
            alpha = jnp.exp2(m_prev - m_new)
            p = jnp.exp2(s - jnp.concatenate([m_new] * tiles, axis=1)).astype(BF16)
            v_ext = jnp.concatenate([v_ref[pl.ds(start, width), :], jnp.ones((width, dv), BF16)], axis=1)
            acc_ref[g] = (jnp.concatenate([alpha, alpha], axis=1) * acc_ref[g]
                          + jnp.dot(p, v_ext, preferred_element_type=F32))
            m_ref[g] = m_new

    def finish():
        for g in range(ATT_GROUPS):
            acc = acc_ref[g]
            o = acc[:, :dv] / acc[:, dv:]
            z = z_ref[g * rows:(g + 1) * rows, :].astype(F32)
            o_ref[g * rows:(g + 1) * rows, :] = (o * _silu(z)).astype(o_ref.dtype)

    m_ref[...] = jnp.full(m_ref.shape, -jnp.inf, F32)
    acc_ref[...] = jnp.zeros(acc_ref.shape, F32)
    scores(sa_ref, 0)

    def pair(jj, carry):
        blk = 2 * jj
        scores(sb_ref, blk + 1)
        softmax_pv(sa_ref, blk, False)
        scores(sa_ref, blk + 2)
        softmax_pv(sb_ref, blk + 1, False)
        return carry

    lax.fori_loop(0, i // 2, pair, 0)

    @pl.when(i % 2 == 0)
    def _():
        softmax_pv(sa_ref, i, True)
        finish()

    @pl.when(i % 2 == 1)
    def _():
        scores(sb_ref, i)
        softmax_pv(sa_ref, i - 1, False)
        softmax_pv(sb_ref, i, True)
        finish()


def _attention(q, k, v, z, tq=ATT_TQ):
    s = q.shape[0]
    tq = min(tq, s)
    tk = tq
    rows = tq // ATT_GROUPS
    return pl.pallas_call(
        functools.partial(_attn_kernel, tk=tk), grid=(HEADS, s // tq),
        in_specs=[pl.BlockSpec((tq, MLA_QK_PAD), lambda h, i: (i, h)),
                  pl.BlockSpec((s, MLA_QK_PAD), lambda h, i: (0, h)),
                  pl.BlockSpec((s, HEAD_DIM), lambda h, i: (0, h)),
                  pl.BlockSpec((tq, HEAD_DIM), lambda h, i: (i, h))],
        out_specs=pl.BlockSpec((tq, HEAD_DIM), lambda h, i: (i, h)),
        out_shape=jax.ShapeDtypeStruct((s, HEADS * HEAD_DIM), BF16),
        scratch_shapes=[pltpu.VMEM((tq, tk), F32), pltpu.VMEM((tq, tk), F32),
                        pltpu.VMEM((ATT_GROUPS, rows, HEAD_DIM), F32),
                        pltpu.VMEM((ATT_GROUPS, rows, 2 * HEAD_DIM), F32)],
        compiler_params=_params("parallel", "arbitrary"), name="mla_attention",
    )(q, k, v, z)


def _gdn_branch(h, w_in, w_conv, a_log, dt_bias, norm_g, w_out):
    width = 4 * HEADS * HEAD_DIM
    w_in_t = w_in.T
    proj = _matmul_w32(h, w_in_t, width, BF16, *IN_PROJ_BLOCK, nt=True)
    w_gate = jnp.pad(w_in_t[width:], ((0, 128 - 2 * HEADS), (0, 0))).astype(BF16)
    ab = _matmul(h, w_gate, F32, *GATE_PROJ_BLOCK, nt=True)[:, :2 * HEADS]
    beta, gc = _gdn_gates(ab.T, a_log, dt_bias, GDN_CHUNK)
    gated = _gdn_core(proj, gc, beta, w_conv, norm_g)
    return _matmul_w32(gated, w_out, w_out.shape[1], BF16, *OUT_PROJ_BLOCK)


def _mla_branch(h, positions, w_in, q_norm_g, w_qb, kv_norm_g, w_kvb, w_out):
    s = h.shape[0]
    n_lat = MLA_Q_RANK + MLA_KV_RANK + MLA_ROPE
    w_in_t = w_in.T
    w_lat = jnp.pad(w_in_t[:n_lat], ((0, HEAD_DIM - MLA_ROPE), (0, 0))).astype(BF16)
    cq, ckv, k_rope = _latent_proj(h, w_lat)
    z = _matmul(h, w_in_t[n_lat:].astype(BF16), BF16, *Z_PROJ_BLOCK, nt=True)

    wq = w_qb.reshape(MLA_Q_RANK, HEADS, MLA_QK)
    wq = jnp.pad(wq, ((0, 0), (0, 0), (0, MLA_QK_PAD - MLA_QK))).reshape(MLA_Q_RANK, HEADS * MLA_QK_PAD)
    wkv = w_kvb.reshape(MLA_KV_RANK, HEADS, 2 * HEAD_DIM)
    wk = wkv[:, :, :HEAD_DIM].reshape(MLA_KV_RANK, HEADS * HEAD_DIM)
    wv = wkv[:, :, HEAD_DIM:].reshape(MLA_KV_RANK, HEADS * HEAD_DIM)

    cm, sn, sp = _rope_tables(positions.reshape(s))
    q = _q_proj(cq, q_norm_g.reshape(1, -1), wq.astype(BF16), (cm, sn, sp))
    k, v = _kv_proj(ckv, kv_norm_g.reshape(1, -1), wk.astype(BF16), wv.astype(BF16), (k_rope, cm, sn, sp))
    gated = _attention(q, k, v, z)
    return _matmul_w32(gated, w_out, w_out.shape[1], BF16, *OUT_PROJ_BLOCK)


def kernel(x, c, positions, w_mod, b_mod, ln_g, ln_b, a_w_in, a_w_conv, a_a_log, a_dt_bias, a_norm_g, a_w_out,
           b_w_in, b_q_norm_g, b_w_qb, b_kv_norm_g, b_w_kvb, b_w_out):
    bsz, s, d = x.shape
    assert bsz == 1 and d == D_MODEL and DEPTH == 2
    x0 = x.reshape(s, d)
    mod = _modulation(c, w_mod, b_mod)
    vec = lambda t: t.reshape(1, d)

    h0 = _modulate(x0, mod[0, 1], mod[0, 0])
    y0 = _gdn_branch(h0, a_w_in[0], a_w_conv[0], a_a_log[0], a_dt_bias[0], a_norm_g[0], a_w_out[0])
    x1, h1 = _residual_layernorm(x0, y0, mod[0, 2], vec(ln_g[0]), vec(ln_b[0]), mod[1, 1], mod[1, 0])
    y1 = _mla_branch(h1, positions, b_w_in[0], b_q_norm_g[0], b_w_qb[0], b_kv_norm_g[0], b_w_kvb[0], b_w_out[0])
    x2 = _residual_layernorm(x1, y1, mod[1, 2], vec(ln_g[1]), vec(ln_b[1]))
    return x2.reshape(bsz, s, d)
```

```python
import functools

import jax
import jax.numpy as jnp
from jax import lax
from jax.experimental import pallas as pl
from jax.experimental.pallas import tpu as pltpu

F32 = jnp.float32
BF16 = jnp.bfloat16

D_MODEL = 4096
DEPTH = 2
DEEPNORM_ALPHA = (2.0 * DEPTH) ** 0.25
RMS_EPS = 1e-6
LN_EPS = 1e-5

HEADS = 32
HEAD_DIM = 128
GDN_CONV = 4
GDN_CHUNK = 128
GDN_BLOCK = 1024
GDN_HEADS_PER_STEP = 4
GDN_INV_BASE = 32
CONV_HALO = 8

MLA_Q_RANK = 896
MLA_KV_RANK = 512
MLA_ROPE = 64
MLA_QK = HEAD_DIM + MLA_ROPE
MLA_QK_PAD = 256
ROPE_THETA = 10000.0
LOG2_E = 1.4426950408889634
ATT_TQ = 1024
ATT_GROUPS = 4

VMEM_LIMIT = 56 * 1024 * 1024

IN_PROJ_BLOCK = (512, 1024)
OUT_PROJ_BLOCK = (1024, 512)
Z_PROJ_BLOCK = (1024, 1024)
GATE_PROJ_BLOCK = (1024, 128)


def _params(*sem):
    return pltpu.CompilerParams(dimension_semantics=sem, vmem_limit_bytes=VMEM_LIMIT)


def _silu(x):
    return x / (1.0 + jnp.exp(-x))


def _mod_kernel(c_ref, w_ref, b_ref, o_ref):
    tn = o_ref.shape[-1]
    rows = 128

    def body(kk, acc):
        r = pl.multiple_of(kk * rows, rows)
        c = c_ref[pl.ds(r, rows), :]
        a = _silu(c)
        p = w_ref[pl.ds(r, rows), :] * jnp.concatenate([a] * (tn // 128), axis=1)
        for t in range(rows // 8):
            acc = acc + p[t * 8:(t + 1) * 8, :]
        return acc

    acc = lax.fori_loop(0, c_ref.shape[0] // rows, body, jnp.zeros((8, tn), F32))
    o_ref[...] = jnp.sum(acc, axis=0, keepdims=True) + b_ref[...]


def _modulation(c, w_mod, b_mod, tn=512):
    depth, d, n = w_mod.shape
    c_rep = jnp.broadcast_to(c.reshape(d, 1), (d, 128))
    out = pl.pallas_call(
        _mod_kernel,
        grid=(depth, n // tn),
        in_specs=[pl.BlockSpec((d, 128), lambda l, j: (0, 0)),
                  pl.BlockSpec((None, d, tn), lambda l, j: (l, 0, j)),
                  pl.BlockSpec((None, 1, tn), lambda l, j: (l, 0, j))],
        out_specs=pl.BlockSpec((None, 1, tn), lambda l, j: (l, 0, j)),
        out_shape=jax.ShapeDtypeStruct((depth, 1, n), F32),
        compiler_params=_params("parallel", "parallel"),
        name="adaln_mod",
    )(c_rep, w_mod, b_mod.reshape(depth, 1, n))
    return out.reshape(depth, 3, 1, d)


def _modulate_kernel(x_ref, sc_ref, sh_ref, h_ref):
    h_ref[...] = (x_ref[...] * (1.0 + sc_ref[...]) + sh_ref[...]).astype(h_ref.dtype)


def _modulate(x, scale, shift, bm=512):
    s, d = x.shape
    row = pl.BlockSpec((bm, d), lambda i: (i, 0))
    vec = pl.BlockSpec((1, d), lambda i: (0, 0))
    return pl.pallas_call(
        _modulate_kernel, grid=(s // bm,), in_specs=[row, vec, vec], out_specs=row,
        out_shape=jax.ShapeDtypeStruct((s, d), BF16),
        compiler_params=_params("parallel"), name="modulate",
    )(x, scale, shift)


def _resid_ln(x_ref, y_ref, gate_ref, g_ref, b_ref):
    t = DEEPNORM_ALPHA * x_ref[...] + (1.0 + gate_ref[...]) * y_ref[...].astype(F32)
    mu = jnp.mean(t, axis=-1, keepdims=True)
    dlt = t - mu
    var = jnp.mean(dlt * dlt, axis=-1, keepdims=True)
    return dlt * lax.rsqrt(var + LN_EPS) * g_ref[...] + b_ref[...]


def _resid_ln_kernel(x_ref, y_ref, gate_ref, g_ref, b_ref, xo_ref):
    xo_ref[...] = _resid_ln(x_ref, y_ref, gate_ref, g_ref, b_ref)


def _resid_ln_mod_kernel(x_ref, y_ref, gate_ref, g_ref, b_ref, sc_ref, sh_ref, xo_ref, h_ref):
    xn = _resid_ln(x_ref, y_ref, gate_ref, g_ref, b_ref)
    xo_ref[...] = xn
    h_ref[...] = (xn * (1.0 + sc_ref[...]) + sh_ref[...]).astype(h_ref.dtype)


def _residual_layernorm(x, y, gate, ln_g, ln_b, next_scale=None, next_shift=None, bm=256):
    s, d = x.shape
    row = pl.BlockSpec((bm, d), lambda i: (i, 0))
    vec = pl.BlockSpec((1, d), lambda i: (0, 0))
    if next_scale is None:
        return pl.pallas_call(
            _resid_ln_kernel, grid=(s // bm,), in_specs=[row, row, vec, vec, vec], out_specs=row,
            out_shape=jax.ShapeDtypeStruct((s, d), F32),
            compiler_params=_params("parallel"), name="resid_ln",
        )(x, y, gate, ln_g, ln_b)
    return pl.pallas_call(
        _resid_ln_mod_kernel, grid=(s // bm,), in_specs=[row, row, vec, vec, vec, vec, vec],
        out_specs=[row, row],
        out_shape=[jax.ShapeDtypeStruct((s, d), F32), jax.ShapeDtypeStruct((s, d), BF16)],
        compiler_params=_params("parallel"), name="resid_ln_mod",
    )(x, y, gate, ln_g, ln_b, next_scale, next_shift)


_NT_DIMS = (((1,), (1,)), ((), ()))


def _dot(a, b, nt):
    if nt:
        return lax.dot_general(a, b, _NT_DIMS, preferred_element_type=F32)
    return jnp.dot(a, b, preferred_element_type=F32)


def _weight_spec(k, bn, nt):
    if nt:
        return pl.BlockSpec((bn, k), lambda j, i: (j, 0))
    return pl.BlockSpec((k, bn), lambda j, i: (0, j))


def _mm_kernel(a_ref, b_ref, o_ref, *, nt):
    o_ref[...] = _dot(a_ref[...], b_ref[...], nt).astype(o_ref.dtype)


def _matmul(a, b, out_dtype, bm, bn, nt=False):
    m, k = a.shape
    n = b.shape[0] if nt else b.shape[1]
    bm = min(bm, m)
    return pl.pallas_call(
        functools.partial(_mm_kernel, nt=nt), grid=(n // bn, m // bm),
        in_specs=[pl.BlockSpec((bm, k), lambda j, i: (i, 0)), _weight_spec(k, bn, nt)],
        out_specs=pl.BlockSpec((bm, bn), lambda j, i: (i, j)),
        out_shape=jax.ShapeDtypeStruct((m, n), out_dtype),
        compiler_params=_params("parallel", "parallel"), name="matmul",
    )(a, b)


def _mm_w32_kernel(a_ref, w_ref, o_ref, wb_ref, *, nt):
    @pl.when(pl.program_id(1) == 0)
    def _():
        wb_ref[...] = w_ref[...].astype(BF16)

    o_ref[...] = _dot(a_ref[...], wb_ref[...], nt).astype(o_ref.dtype)


def _matmul_w32(a, w, n, out_dtype, bm, bn, nt=False):
    m, k = a.shape
    bm = min(bm, m)
    return pl.pallas_call(
        functools.partial(_mm_w32_kernel, nt=nt), grid=(n // bn, m // bm),
        in_specs=[pl.BlockSpec((bm, k), lambda j, i: (i, 0)), _weight_spec(k, bn, nt)],
        out_specs=pl.BlockSpec((bm, bn), lambda j, i: (i, j)),
        out_shape=jax.ShapeDtypeStruct((m, n), out_dtype),
        scratch_shapes=[pltpu.VMEM((bn, k) if nt else (k, bn), BF16)],
        compiler_params=_params("parallel", "arbitrary"), name="matmul_w32",
    )(a, w)


def _rms_rows(a, g):
    return a * lax.rsqrt(jnp.mean(a * a, axis=-1, keepdims=True) + RMS_EPS) * g


def _gdn_gate_kernel(ab_ref, alog_ref, dtb_ref, beta_ref, gc_ref, *, chunk):
    b_raw = ab_ref[0:HEADS, :]
    a_raw = ab_ref[HEADS:2 * HEADS, :]
    beta_ref[...] = 1.0 / (1.0 + jnp.exp(-b_raw))
    xx = a_raw + dtb_ref[...]
    softplus = jnp.maximum(xx, 0.0) + jnp.log(1.0 + jnp.exp(-jnp.abs(xx)))
    g = -jnp.exp(alog_ref[...]) * softplus
    pos = lax.broadcasted_iota(jnp.int32, g.shape, 1) % chunk
    sh = 1
    while sh < chunk:
        g = g + jnp.where(pos >= sh, pltpu.roll(g, sh, axis=1), 0.0)
        sh *= 2
    gc_ref[...] = g


def _gdn_gates(ab_t, a_log, dt_bias, chunk, tn=2048):
    two_h, s = ab_t.shape
    tn = min(tn, s)
    col = pl.BlockSpec((HEADS, 1), lambda j: (0, 0))
    out = pl.BlockSpec((HEADS, tn), lambda j: (0, j))
    return pl.pallas_call(
        functools.partial(_gdn_gate_kernel, chunk=chunk), grid=(s // tn,),
        in_specs=[pl.BlockSpec((two_h, tn), lambda j: (0, j)), col, col],
        out_specs=[out, out],
        out_shape=[jax.ShapeDtypeStruct((HEADS, s), F32)] * 2,
        compiler_params=_params("parallel"), name="gdn_gates",
    )(ab_t, a_log.reshape(HEADS, 1), dt_bias.reshape(HEADS, 1))


def _bmm(a, b):
    return jnp.einsum("nik,nkj->nij", a.astype(BF16), b.astype(BF16), preferred_element_type=F32)


def _unit_lower_inverse(a_mat, row, col, eye, base):
    size = a_mat.shape[-1]
    same = lambda b: (row // b) == (col // b)
    a0 = jnp.where(same(base), a_mat, 0.0)
    p = eye - a0
    m = _bmm(a0, a0)
    n = 1
    while 2 * n < base // 2:
        pm = _bmm(jnp.concatenate([p, m], axis=1), m)
        p = p + pm[:, :size]
        m = pm[:, size:]
        n += n
    p = p + _bmm(p, m)
    b = base
    while b < size:
        a_off = jnp.where(same(2 * b) & jnp.logical_not(same(b)), a_mat, 0.0)
        p = p - _bmm(_bmm(p, a_off), p)
        b *= 2
    return p


def _gdn_core_kernel(q_ref, k_ref, v_ref, z_ref, gc_ref, beta_ref, wq_ref, wk_ref, wv_ref, ng_ref,
                     o_ref, state_ref, qbuf, kbuf, vbuf, u_s, wq_s, qk_s, kdt_s, gl_s, *, chunk):
    bt = q_ref.shape[0]
    dk = HEAD_DIM
    nh = q_ref.shape[1] // dk
    nc = bt // chunk
    halo = CONV_HALO

    @pl.when(pl.program_id(1) == 0)
    def _():
        state_ref[...] = jnp.zeros_like(state_ref)
        for buf in (qbuf, kbuf, vbuf):
            buf[0:halo, :] = jnp.zeros((halo, nh * dk), F32)

    qbuf[halo:halo + bt, :] = q_ref[...].astype(F32)
    kbuf[halo:halo + bt, :] = k_ref[...].astype(F32)
    vbuf[halo:halo + bt, :] = v_ref[...].astype(F32)

    row = lax.broadcasted_iota(jnp.int32, (chunk, chunk), 0)
    col = lax.broadcasted_iota(jnp.int32, (chunk, chunk), 1)
    incl = row >= col
    strict = row > col
    eye = (row == col).astype(F32)

    def conv_silu(buf, w_ref, lanes):
        w = w_ref[:, lanes]
        acc = buf[halo:halo + bt, lanes] * w[GDN_CONV - 1:GDN_CONV, :]
        for j in range(GDN_CONV - 1):
            sh = GDN_CONV - 1 - j
            acc = acc + buf[halo - sh:halo - sh + bt, lanes] * w[j:j + 1, :]
        return _silu(acc)

    def l2n(x, scale=1.0):
        return x * (lax.rsqrt(jnp.sum(x * x, axis=-1, keepdims=True) + RMS_EPS) * scale)

    def chunk_rows(ref, hh):
        tiles = [jnp.broadcast_to(ref[hh:hh + 1, c * chunk:(c + 1) * chunk], (chunk, chunk)) for c in range(nc)]
        return jnp.stack(tiles), jnp.stack([t.T for t in tiles])

    def phase_a(hh):
        lanes = slice(hh * dk, (hh + 1) * dk)
        q = l2n(conv_silu(qbuf, wq_ref, lanes), dk ** -0.5).reshape(nc, chunk, dk)
        k = l2n(conv_silu(kbuf, wk_ref, lanes)).reshape(nc, chunk, dk)
        v = conv_silu(vbuf, wv_ref, lanes).reshape(nc, chunk, dk)

        gc_rb, gc_cb = chunk_rows(gc_ref, hh)
        _, bt_cb = chunk_rows(beta_ref, hh)
        decay = jnp.exp(jnp.where(incl, gc_cb - gc_rb, -jnp.inf))

        qkk = jnp.einsum("nid,njd->nij", jnp.concatenate([q, k], axis=1).astype(BF16), k.astype(BF16),
                         preferred_element_type=F32)
        qk = qkk[:, :chunk] * decay
        a_mat = jnp.where(strict, qkk[:, chunk:] * bt_cb * decay, 0.0)
        t_mat = _unit_lower_inverse(a_mat, row, col, eye, GDN_INV_BASE)

        e_gc = jnp.exp(gc_cb)
        uw = _bmm(t_mat, jnp.concatenate([v * bt_cb, k * (bt_cb * e_gc)], axis=2))
        gl_row = gc_cb[:, chunk - 1:chunk, :]
        k_dec = k * jnp.exp(gl_row - gc_cb)

        u_s[:, hh] = uw[:, :, :dk]
        wq_s[:, hh] = jnp.concatenate([uw[:, :, dk:], q * e_gc], axis=1).astype(BF16)
        qk_s[:, hh] = qk.astype(BF16)
        kdt_s[:, hh] = jnp.stack([k_dec[c].T for c in range(nc)]).astype(BF16)
        gl_s[:, hh] = jnp.broadcast_to(jnp.exp(gl_row), (nc, 8, dk))

    for hh in range(nh):
        phase_a(hh)

    def phase_b(c, state):
        ws = _bmm(wq_s[c], state)
        v_new = u_s[c] - ws[:, :chunk]
        vb = v_new.astype(BF16)
        o = ws[:, chunk:] + _bmm(qk_s[c], vb)
        state = state * gl_s[c][:, 0:1, :] + _bmm(kdt_s[c], vb)
        on = o * lax.rsqrt(jnp.mean(o * o, axis=-1, keepdims=True) + RMS_EPS) * ng_ref[...]
        r0 = pl.multiple_of(c * chunk, chunk)
        z = z_ref[pl.ds(r0, chunk), :].astype(F32)
        for hh in range(nh):
            lanes = slice(hh * dk, (hh + 1) * dk)
            o_ref[pl.ds(r0, chunk), lanes] = (on[hh] * _silu(z[:, lanes])).astype(o_ref.dtype)
        return state

    state_ref[...] = lax.fori_loop(0, nc, phase_b, state_ref[...], unroll=4)
    for buf in (qbuf, kbuf, vbuf):
        buf[0:halo, :] = buf[bt:bt + halo, :]


def _gdn_core(proj, gc, beta, w_conv, norm_g, chunk=GDN_CHUNK, bt=GDN_BLOCK, nh=GDN_HEADS_PER_STEP):
    s = proj.shape[0]
    bt = min(bt, s)
    dk = HEAD_DIM
    nc = bt // chunk
    groups = HEADS // nh
    tok = lambda part: pl.BlockSpec((bt, nh * dk), lambda h, b: (b, part * groups + h))
    gate = pl.BlockSpec((None, nh, bt), lambda h, b: (h, 0, b))
    cw = lambda part: pl.BlockSpec((GDN_CONV, nh * dk), lambda h, b: (0, part * groups + h))
    return pl.pallas_call(
        functools.partial(_gdn_core_kernel, chunk=chunk),
        grid=(groups, s // bt),
        in_specs=[tok(0), tok(1), tok(2), tok(3), gate, gate,
                  cw(0), cw(1), cw(2), pl.BlockSpec((1, dk), lambda h, b: (0, 0))],
        out_specs=pl.BlockSpec((bt, nh * dk), lambda h, b: (b, h)),
        out_shape=jax.ShapeDtypeStruct((s, HEADS * dk), BF16),
        scratch_shapes=[pltpu.VMEM((nh, dk, dk), F32)] + [pltpu.VMEM((bt + 2 * CONV_HALO, nh * dk), F32)] * 3
        + [pltpu.VMEM((nc, nh, chunk, dk), F32), pltpu.VMEM((nc, nh, 2 * chunk, dk), BF16),
           pltpu.VMEM((nc, nh, chunk, chunk), BF16), pltpu.VMEM((nc, nh, dk, chunk), BF16),
           pltpu.VMEM((nc, nh, 8, dk), F32)],
        compiler_params=_params("parallel", "arbitrary"), name="gdn_core",
    )(proj, proj, proj, proj, gc.reshape(groups, nh, s), beta.reshape(groups, nh, s),
      w_conv, w_conv, w_conv, norm_g.reshape(1, dk))


def _rope_tab_kernel(pos_ref, invf_ref, cm_ref, sn_ref, sp_ref):
    ang = pos_ref[...].astype(F32) * invf_ref[...]
    cos = jnp.cos(ang)
    sin = jnp.sin(ang)
    lane = lax.broadcasted_iota(jnp.int32, ang.shape, 1)
    half = MLA_ROPE // 2
    cm_ref[...] = jnp.where(lane < MLA_ROPE, cos, 0.0)
    sn_ref[...] = jnp.where(lane < half, -sin, 0.0)
    sp_ref[...] = jnp.where((lane >= half) & (lane < MLA_ROPE), sin, 0.0)


def _rope_tables(positions, bm=512):
    s = positions.shape[0]
    bm = min(bm, s)
    half = MLA_ROPE // 2
    inv_freq = ROPE_THETA ** (-jnp.arange(0, half, dtype=F32) / half)
    invf = jnp.concatenate([inv_freq, inv_freq, jnp.zeros((HEAD_DIM - MLA_ROPE,), F32)]).reshape(1, HEAD_DIM)
    tab = pl.BlockSpec((bm, HEAD_DIM), lambda i: (i, 0))
    return pl.pallas_call(
        _rope_tab_kernel, grid=(s // bm,),
        in_specs=[pl.BlockSpec((bm, 1), lambda i: (i, 0)), pl.BlockSpec((1, HEAD_DIM), lambda i: (0, 0))],
        out_specs=[tab, tab, tab],
        out_shape=[jax.ShapeDtypeStruct((s, HEAD_DIM), F32)] * 3,
        compiler_params=_params("parallel"), name="rope_tables",
    )(positions.reshape(s, 1), invf)


def _rope(x, cm, sn, sp):
    half = MLA_ROPE // 2
    return x * cm + pltpu.roll(x, HEAD_DIM - half, axis=1) * sn + pltpu.roll(x, half, axis=1) * sp


def _q_proj_kernel(a_ref, g_ref, w_ref, cm_ref, sn_ref, sp_ref, o_ref):
    an = _rms_rows(a_ref[...], g_ref[...]).astype(BF16)
    acc = jnp.dot(an, w_ref[...], preferred_element_type=F32)
    scale = MLA_QK ** -0.5 * LOG2_E
    cm, sn, sp = cm_ref[...] * scale, sn_ref[...] * scale, sp_ref[...] * scale
    for g in range(o_ref.shape[1] // MLA_QK_PAD):
        lo = g * MLA_QK_PAD
        hi = lo + HEAD_DIM
        o_ref[:, lo:hi] = (acc[:, lo:hi] * scale).astype(o_ref.dtype)
        o_ref[:, hi:hi + HEAD_DIM] = _rope(acc[:, hi:hi + HEAD_DIM], cm, sn, sp).astype(o_ref.dtype)


def _kv_proj_kernel(a_ref, g_ref, wk_ref, wv_ref, kr_ref, cm_ref, sn_ref, sp_ref, k_ref, v_ref):
    an = _rms_rows(a_ref[...], g_ref[...]).astype(BF16)
    v_ref[...] = jnp.dot(an, wv_ref[...], preferred_element_type=F32).astype(v_ref.dtype)
    acc = jnp.dot(an, wk_ref[...], preferred_element_type=F32)
    kr = _rope(kr_ref[...], cm_ref[...], sn_ref[...], sp_ref[...]).astype(k_ref.dtype)
    for g in range(k_ref.shape[1] // MLA_QK_PAD):
        lo = g * MLA_QK_PAD
        k_ref[:, lo:lo + HEAD_DIM] = acc[:, g * HEAD_DIM:(g + 1) * HEAD_DIM].astype(k_ref.dtype)
        k_ref[:, lo + HEAD_DIM:lo + MLA_QK_PAD] = kr


def _q_proj(a, g, w, tables, bm=1024, heads_per_step=8):
    m, k = a.shape
    bm = min(bm, m)
    bn = heads_per_step * MLA_QK_PAD
    tab = pl.BlockSpec((bm, HEAD_DIM), lambda j, i: (i, 0))
    return pl.pallas_call(
        _q_proj_kernel, grid=(HEADS // heads_per_step, m // bm),
        in_specs=[pl.BlockSpec((bm, k), lambda j, i: (i, 0)),
                  pl.BlockSpec((1, k), lambda j, i: (0, 0)),
                  pl.BlockSpec((k, bn), lambda j, i: (0, j))] + [tab] * len(tables),
        out_specs=pl.BlockSpec((bm, bn), lambda j, i: (i, j)),
        out_shape=jax.ShapeDtypeStruct((m, HEADS * MLA_QK_PAD), BF16),
        compiler_params=_params("parallel", "parallel"), name="q_proj",
    )(a, g, w, *tables)


def _kv_proj(a, g, wk, wv, tables, bm=1024, heads_per_step=8):
    m, k = a.shape
    bm = min(bm, m)
    bw = heads_per_step * HEAD_DIM
    tab = pl.BlockSpec((bm, HEAD_DIM), lambda j, i: (i, 0))
    wspec = pl.BlockSpec((k, bw), lambda j, i: (0, j))
    return pl.pallas_call(
        _kv_proj_kernel, grid=(HEADS // heads_per_step, m // bm),
        in_specs=[pl.BlockSpec((bm, k), lambda j, i: (i, 0)),
                  pl.BlockSpec((1, k), lambda j, i: (0, 0)), wspec, wspec] + [tab] * len(tables),
        out_specs=[pl.BlockSpec((bm, heads_per_step * MLA_QK_PAD), lambda j, i: (i, j)),
                   pl.BlockSpec((bm, bw), lambda j, i: (i, j))],
        out_shape=[jax.ShapeDtypeStruct((m, HEADS * MLA_QK_PAD), BF16),
                   jax.ShapeDtypeStruct((m, HEADS * HEAD_DIM), BF16)],
        compiler_params=_params("parallel", "parallel"), name="kv_proj",
    )(a, g, wk, wv, *tables)


def _latent_kernel(a_ref, w_ref, cq_ref, ckv_ref, kr_ref):
    acc = _dot(a_ref[...], w_ref[...], True)
    cq_ref[...] = acc[:, :MLA_Q_RANK]
    ckv_ref[...] = acc[:, MLA_Q_RANK:MLA_Q_RANK + MLA_KV_RANK]
    kr_ref[...] = acc[:, MLA_Q_RANK + MLA_KV_RANK:]


def _latent_proj(h, w_lat, bm=512):
    m, k = h.shape
    n = w_lat.shape[0]
    bm = min(bm, m)
    widths = (MLA_Q_RANK, MLA_KV_RANK, n - MLA_Q_RANK - MLA_KV_RANK)
    return pl.pallas_call(
        _latent_kernel, grid=(m // bm,),
        in_specs=[pl.BlockSpec((bm, k), lambda i: (i, 0)), pl.BlockSpec((n, k), lambda i: (0, 0))],
        out_specs=[pl.BlockSpec((bm, wd), lambda i: (i, 0)) for wd in widths],
        out_shape=[jax.ShapeDtypeStruct((m, wd), F32) for wd in widths],
        compiler_params=_params("parallel"), name="latent_proj",
    )(h, w_lat)


def _attn_kernel(q_ref, k_ref, v_ref, z_ref, o_ref, sa_ref, sb_ref, m_ref, acc_ref, *, tk):
    i = pl.program_id(1)
    tq = q_ref.shape[0]
    dv = v_ref.shape[1]
    rows = tq // ATT_GROUPS

    def scores(s_ref, blk):
        k_blk = k_ref[pl.ds(pl.multiple_of(blk * tk, tk), tk), :]
        s_ref[...] = lax.dot_general(q_ref[...], k_blk, (((1,), (1,)), ((), ())), preferred_element_type=F32)

    def softmax_pv(s_ref, blk, masked):
        start = pl.multiple_of(blk * tk, tk)
        for g in range(ATT_GROUPS):
            width = (g + 1) * rows if masked else tk
            tiles = width // dv
            s = s_ref[g * rows:(g + 1) * rows, 0:width]
            if masked:
                row = lax.broadcasted_iota(jnp.int32, (rows, width), 0) + g * rows
                col = lax.broadcasted_iota(jnp.int32, (rows, width), 1)
                s = jnp.where(col <= row, s, -jnp.inf)
            m_part = s[:, :dv]
            for t in range(1, tiles):
                m_part = jnp.maximum(m_part, s[:, t * dv:(t + 1) * dv])
            m_prev = m_ref[g]
            m_new = jnp.maximum(m_prev, jnp.broadcast_to(jnp.max(m_part, axis=-1, keepdims=True), (rows, dv)))
            alpha = jnp.exp2(m_prev - m_new)
            p = jnp.exp2(s - jnp.concatenate([m_new] * tiles, axis=1)).astype(BF16)
            v_ext = jnp.concatenate([v_ref[pl.ds(start, width), :], jnp.ones((width, dv), BF16)], axis=1)
            acc_ref[g] = (jnp.concatenate([alpha, alpha], axis=1) * acc_ref[g]
                          + jnp.dot(p, v_ext, preferred_element_type=F32))
            m_ref[g] = m_new

    def finish():
        for g in range(ATT_GROUPS):
            acc = acc_ref[g]
            o = acc[:, :dv] / acc[:, dv:]
            z = z_ref[g * rows:(g + 1) * rows, :].astype(F32)
            o_ref[g * rows:(g + 1) * rows, :] = (o * _silu(z)).astype(o_ref.dtype)

    m_ref[...] = jnp.full(m_ref.shape, -jnp.inf, F32)
    acc_ref[...] = jnp.zeros(acc_ref.shape, F32)
    scores(sa_ref, 0)

    def pair(jj, carry):
        blk = 2 * jj
        scores(sb_ref, blk + 1)
        softmax_pv(sa_ref, blk, False)
        scores(sa_ref, blk + 2)
        softmax_pv(sb_ref, blk + 1, False)
        return carry

    lax.fori_loop(0, i // 2, pair, 0)

    @pl.when(i % 2 == 0)
    def _():
        softmax_pv(sa_ref, i, True)
        finish()

    @pl.when(i % 2 == 1)
    def _():
        scores(sb_ref, i)
        softmax_pv(sa_ref, i - 1, False)
        softmax_pv(sb_ref, i, True)
        finish()


def _attention(q, k, v, z, tq=ATT_TQ):
    s = q.shape[0]
    tq = min(tq, s)
    tk = tq
    rows = tq // ATT_GROUPS
    return pl.pallas_call(
        functools.partial(_attn_kernel, tk=tk), grid=(HEADS, s // tq),
        in_specs=[pl.BlockSpec((tq, MLA_QK_PAD), lambda h, i: (i, h)),
                  pl.BlockSpec((s, MLA_QK_PAD), lambda h, i: (0, h)),
                  pl.BlockSpec((s, HEAD_DIM), lambda h, i: (0, h)),
                  pl.BlockSpec((tq, HEAD_DIM), lambda h, i: (i, h))],
        out_specs=pl.BlockSpec((tq, HEAD_DIM), lambda h, i: (i, h)),
        out_shape=jax.ShapeDtypeStruct((s, HEADS * HEAD_DIM), BF16),
        scratch_shapes=[pltpu.VMEM((tq, tk), F32), pltpu.VMEM((tq, tk), F32),
                        pltpu.VMEM((ATT_GROUPS, rows, HEAD_DIM), F32),
                        pltpu.VMEM((ATT_GROUPS, rows, 2 * HEAD_DIM), F32)],
        compiler_params=_params("parallel", "arbitrary"), name="mla_attention",
    )(q, k, v, z)


def _gdn_branch(h, w_in, w_conv, a_log, dt_bias, norm_g, w_out):
    width = 4 * HEADS * HEAD_DIM
    w_in_t = w_in.T
    proj = _matmul_w32(h, w_in_t, width, BF16, *IN_PROJ_BLOCK, nt=True)
    w_gate = jnp.pad(w_in_t[width:], ((0, 128 - 2 * HEADS), (0, 0))).astype(BF16)
    ab = _matmul(h, w_gate, F32, *GATE_PROJ_BLOCK, nt=True)[:, :2 * HEADS]
    beta, gc = _gdn_gates(ab.T, a_log, dt_bias, GDN_CHUNK)
    gated = _gdn_core(proj, gc, beta, w_conv, norm_g)
    return _matmul_w32(gated, w_out, w_out.shape[1], BF16, *OUT_PROJ_BLOCK)


def _mla_branch(h, positions, w_in, q_norm_g, w_qb, kv_norm_g, w_kvb, w_out):
    s = h.shape[0]
    n_lat = MLA_Q_RANK + MLA_KV_RANK + MLA_ROPE
    w_in_t = w_in.T
    w_lat = jnp.pad(w_in_t[:n_lat], ((0, HEAD_DIM - MLA_ROPE), (0, 0))).astype(BF16)
    cq, ckv, k_rope = _latent_proj(h, w_lat)
    z = _matmul(h, w_in_t[n_lat:].astype(BF16), BF16, *Z_PROJ_BLOCK, nt=True)

    wq = w_qb.reshape(MLA_Q_RANK, HEADS, MLA_QK)
    wq = jnp.pad(wq, ((0, 0), (0, 0), (0, MLA_QK_PAD - MLA_QK))).reshape(MLA_Q_RANK, HEADS * MLA_QK_PAD)
    wkv = w_kvb.reshape(MLA_KV_RANK, HEADS, 2 * HEAD_DIM)
    wk = wkv[:, :, :HEAD_DIM].reshape(MLA_KV_RANK, HEADS * HEAD_DIM)
    wv = wkv[:, :, HEAD_DIM:].reshape(MLA_KV_RANK, HEADS * HEAD_DIM)

    cm, sn, sp = _rope_tables(positions.reshape(s))
    q = _q_proj(cq, q_norm_g.reshape(1, -1), wq.astype(BF16), (cm, sn, sp))
    k, v = _kv_proj(ckv, kv_norm_g.reshape(1, -1), wk.astype(BF16), wv.astype(BF16), (k_rope, cm, sn, sp))
    gated = _attention(q, k, v, z)
    return _matmul_w32(gated, w_out, w_out.shape[1], BF16, *OUT_PROJ_BLOCK)


def kernel(x, c, positions, w_mod, b_mod, ln_g, ln_b, a_w_in, a_w_conv, a_a_log, a_dt_bias, a_norm_g, a_w_out,
           b_w_in, b_q_norm_g, b_w_qb, b_kv_norm_g, b_w_kvb, b_w_out):
    bsz, s, d = x.shape
    assert bsz == 1 and d == D_MODEL and DEPTH == 2
    x0 = x.reshape(s, d)
    mod = _modulation(c, w_mod, b_mod)
    vec = lambda t: t.reshape(1, d)

    h0 = _modulate(x0, mod[0, 1], mod[0, 0])
    y0 = _gdn_branch(h0, a_w_in[0], a_w_conv[0], a_a_log[0], a_dt_bias[0], a_norm_g[0], a_w_out[0])
    x1, h1 = _residual_layernorm(x0, y0, mod[0, 2], vec(ln_g[0]), vec(ln_b[0]), mod[1, 1], mod[1, 0])
    y1 = _mla_branch(h1, positions, b_w_in[0], b_q_norm_g[0], b_w_qb[0], b_kv_norm_g[0], b_w_kvb[0], b_w_out[0])
    x2 = _residual_layernorm(x1, y1, mod[1, 2], vec(ln_g[1]), vec(ln_b[1]))
    return x2.reshape(bsz, s, d)
```

```python
import functools

import jax
import jax.numpy as jnp
from jax import lax
from jax.experimental import pallas as pl
from jax.experimental.pallas import tpu as pltpu

F32 = jnp.float32
BF16 = jnp.bfloat16

D_MODEL = 4096
DEPTH = 2
DEEPNORM_ALPHA = (2.0 * DEPTH) ** 0.25
RMS_EPS = 1e-6
LN_EPS = 1e-5

HEADS = 32
HEAD_DIM = 128
GDN_CONV = 4
GDN_CHUNK = 128
GDN_BLOCK = 1024
GDN_HEADS_PER_STEP = 4
GDN_INV_BASE = 64
CONV_HALO = 8

MLA_Q_RANK = 896
MLA_KV_RANK = 512
MLA_ROPE = 64
MLA_QK = HEAD_DIM + MLA_ROPE
MLA_QK_PAD = 256
ROPE_THETA = 10000.0
LOG2_E = 1.4426950408889634
ATT_TQ = 1024
ATT_GROUPS = 4

VMEM_LIMIT = 56 * 1024 * 1024

IN_PROJ_BLOCK = (512, 1024)
OUT_PROJ_BLOCK = (1024, 512)
Z_PROJ_BLOCK = (1024, 1024)
GATE_PROJ_BLOCK = (1024, 128)


def _params(*sem):
    return pltpu.CompilerParams(dimension_semantics=sem, vmem_limit_bytes=VMEM_LIMIT)


def _silu(x):
    return x / (1.0 + jnp.exp(-x))


def _mod_kernel(c_ref, w_ref, b_ref, o_ref):
    tn = o_ref.shape[-1]
    rows = 128

    def body(kk, acc):
        r = pl.multiple_of(kk * rows, rows)
        c = c_ref[pl.ds(r, rows), :]
        a = _silu(c)
        p = w_ref[pl.ds(r, rows), :] * jnp.concatenate([a] * (tn // 128), axis=1)
        for t in range(rows // 8):
            acc = acc + p[t * 8:(t + 1) * 8, :]
        return acc

    acc = lax.fori_loop(0, c_ref.shape[0] // rows, body, jnp.zeros((8, tn), F32))
    o_ref[...] = jnp.sum(acc, axis=0, keepdims=True) + b_ref[...]


def _modulation(c, w_mod, b_mod, tn=512):
    depth, d, n = w_mod.shape
    c_rep = jnp.broadcast_to(c.reshape(d, 1), (d, 128))
    out = pl.pallas_call(
        _mod_kernel,
        grid=(depth, n // tn),
        in_specs=[pl.BlockSpec((d, 128), lambda l, j: (0, 0)),
                  pl.BlockSpec((None, d, tn), lambda l, j: (l, 0, j)),
                  pl.BlockSpec((None, 1, tn), lambda l, j: (l, 0, j))],
        out_specs=pl.BlockSpec((None, 1, tn), lambda l, j: (l, 0, j)),
        out_shape=jax.ShapeDtypeStruct((depth, 1, n), F32),
        compiler_params=_params("parallel", "parallel"),
        name="adaln_mod",
    )(c_rep, w_mod, b_mod.reshape(depth, 1, n))
    return out.reshape(depth, 3, 1, d)


def _modulate_kernel(x_ref, sc_ref, sh_ref, h_ref):
    h_ref[...] = (x_ref[...] * (1.0 + sc_ref[...]) + sh_ref[...]).astype(h_ref.dtype)


def _modulate(x, scale, shift, bm=512):
    s, d = x.shape
    row = pl.BlockSpec((bm, d), lambda i: (i, 0))
    vec = pl.BlockSpec((1, d), lambda i: (0, 0))
    return pl.pallas_call(
        _modulate_kernel, grid=(s // bm,), in_specs=[row, vec, vec], out_specs=row,
        out_shape=jax.ShapeDtypeStruct((s, d), BF16),
        compiler_params=_params("parallel"), name="modulate",
    )(x, scale, shift)


def _resid_ln(x_ref, y_ref, gate_ref, g_ref, b_ref):
    t = DEEPNORM_ALPHA * x_ref[...] + (1.0 + gate_ref[...]) * y_ref[...].astype(F32)
    mu = jnp.mean(t, axis=-1, keepdims=True)
    dlt = t - mu
    var = jnp.mean(dlt * dlt, axis=-1, keepdims=True)
    return dlt * lax.rsqrt(var + LN_EPS) * g_ref[...] + b_ref[...]


def _resid_ln_kernel(x_ref, y_ref, gate_ref, g_ref, b_ref, xo_ref):
    xo_ref[...] = _resid_ln(x_ref, y_ref, gate_ref, g_ref, b_ref)


def _resid_ln_mod_kernel(x_ref, y_ref, gate_ref, g_ref, b_ref, sc_ref, sh_ref, xo_ref, h_ref):
    xn = _resid_ln(x_ref, y_ref, gate_ref, g_ref, b_ref)
    xo_ref[...] = xn
    h_ref[...] = (xn * (1.0 + sc_ref[...]) + sh_ref[...]).astype(h_ref.dtype)


def _residual_layernorm(x, y, gate, ln_g, ln_b, next_scale=None, next_shift=None, bm=256):
    s, d = x.shape
    row = pl.BlockSpec((bm, d), lambda i: (i, 0))
    vec = pl.BlockSpec((1, d), lambda i: (0, 0))
    if next_scale is None:
        return pl.pallas_call(
            _resid_ln_kernel, grid=(s // bm,), in_specs=[row, row, vec, vec, vec], out_specs=row,
            out_shape=jax.ShapeDtypeStruct((s, d), F32),
            compiler_params=_params("parallel"), name="resid_ln",
        )(x, y, gate, ln_g, ln_b)
    return pl.pallas_call(
        _resid_ln_mod_kernel, grid=(s // bm,), in_specs=[row, row, vec, vec, vec, vec, vec],
        out_specs=[row, row],
        out_shape=[jax.ShapeDtypeStruct((s, d), F32), jax.ShapeDtypeStruct((s, d), BF16)],
        compiler_params=_params("parallel"), name="resid_ln_mod",
    )(x, y, gate, ln_g, ln_b, next_scale, next_shift)


_NT_DIMS = (((1,), (1,)), ((), ()))


def _dot(a, b, nt):
    if nt:
        return lax.dot_general(a, b, _NT_DIMS, preferred_element_type=F32)
    return jnp.dot(a, b, preferred_element_type=F32)


def _weight_spec(k, bn, nt):
    if nt:
        return pl.BlockSpec((bn, k), lambda j, i: (j, 0))
    return pl.BlockSpec((k, bn), lambda j, i: (0, j))


def _mm_kernel(a_ref, b_ref, o_ref, *, nt):
    o_ref[...] = _dot(a_ref[...], b_ref[...], nt).astype(o_ref.dtype)


def _matmul(a, b, out_dtype, bm, bn, nt=False):
    m, k = a.shape
    n = b.shape[0] if nt else b.shape[1]
    bm = min(bm, m)
    return pl.pallas_call(
        functools.partial(_mm_kernel, nt=nt), grid=(n // bn, m // bm),
        in_specs=[pl.BlockSpec((bm, k), lambda j, i: (i, 0)), _weight_spec(k, bn, nt)],
        out_specs=pl.BlockSpec((bm, bn), lambda j, i: (i, j)),
        out_shape=jax.ShapeDtypeStruct((m, n), out_dtype),
        compiler_params=_params("parallel", "parallel"), name="matmul",
    )(a, b)


def _mm_w32_kernel(a_ref, w_ref, o_ref, wb_ref, *, nt):
    @pl.when(pl.program_id(1) == 0)
    def _():
        wb_ref[...] = w_ref[...].astype(BF16)

    o_ref[...] = _dot(a_ref[...], wb_ref[...], nt).astype(o_ref.dtype)


def _matmul_w32(a, w, n, out_dtype, bm, bn, nt=False):
    m, k = a.shape
    bm = min(bm, m)
    return pl.pallas_call(
        functools.partial(_mm_w32_kernel, nt=nt), grid=(n // bn, m // bm),
        in_specs=[pl.BlockSpec((bm, k), lambda j, i: (i, 0)), _weight_spec(k, bn, nt)],
        out_specs=pl.BlockSpec((bm, bn), lambda j, i: (i, j)),
        out_shape=jax.ShapeDtypeStruct((m, n), out_dtype),
        scratch_shapes=[pltpu.VMEM((bn, k) if nt else (k, bn), BF16)],
        compiler_params=_params("parallel", "arbitrary"), name="matmul_w32",
    )(a, w)


def _rms_rows(a, g):
    return a * lax.rsqrt(jnp.mean(a * a, axis=-1, keepdims=True) + RMS_EPS) * g


def _gdn_gate_kernel(ab_ref, alog_ref, dtb_ref, beta_ref, gc_ref, *, chunk):
    b_raw = ab_ref[0:HEADS, :]
    a_raw = ab_ref[HEADS:2 * HEADS, :]
    beta_ref[...] = 1.0 / (1.0 + jnp.exp(-b_raw))
    xx = a_raw + dtb_ref[...]
    softplus = jnp.maximum(xx, 0.0) + jnp.log(1.0 + jnp.exp(-jnp.abs(xx)))
    g = -jnp.exp(alog_ref[...]) * softplus
    pos = lax.broadcasted_iota(jnp.int32, g.shape, 1) % chunk
    sh = 1
    while sh < chunk:
        g = g + jnp.where(pos >= sh, pltpu.roll(g, sh, axis=1), 0.0)
        sh *= 2
    gc_ref[...] = g


def _gdn_gates(ab_t, a_log, dt_bias, chunk, tn=2048):
    two_h, s = ab_t.shape
    tn = min(tn, s)
    col = pl.BlockSpec((HEADS, 1), lambda j: (0, 0))
    out = pl.BlockSpec((HEADS, tn), lambda j: (0, j))
    return pl.pallas_call(
        functools.partial(_gdn_gate_kernel, chunk=chunk), grid=(s // tn,),
        in_specs=[pl.BlockSpec((two_h, tn), lambda j: (0, j)), col, col],
        out_specs=[out, out],
        out_shape=[jax.ShapeDtypeStruct((HEADS, s), F32)] * 2,
        compiler_params=_params("parallel"), name="gdn_gates",
    )(ab_t, a_log.reshape(HEADS, 1), dt_bias.reshape(HEADS, 1))


def _bmm(a, b):
    return jnp.einsum("nik,nkj->nij", a.astype(BF16), b.astype(BF16), preferred_element_type=F32)


def _unit_lower_inverse(a_mat, row, col, eye, base):
    size = a_mat.shape[-1]
    same = lambda b: (row // b) == (col // b)
    a0 = jnp.where(same(base), a_mat, 0.0)
    p = eye - a0
    m = _bmm(a0, a0)
    n = 1
    while 2 * n < base // 2:
        pm = _bmm(jnp.concatenate([p, m], axis=1), m)
        p = p + pm[:, :size]
        m = pm[:, size:]
        n += n
    p = p + _bmm(p, m)
    b = base
    while b < size:
        a_off = jnp.where(same(2 * b) & jnp.logical_not(same(b)), a_mat, 0.0)
        p = p - _bmm(_bmm(p, a_off), p)
        b *= 2
    return p


def _gdn_core_kernel(q_ref, k_ref, v_ref, z_ref, gc_ref, beta_ref, wq_ref, wk_ref, wv_ref, ng_ref,
                     o_ref, state_ref, qbuf, kbuf, vbuf, u_s, wq_s, qk_s, kdt_s, gl_s, *, chunk):
    bt = q_ref.shape[0]
    dk = HEAD_DIM
    nh = q_ref.shape[1] // dk
    nc = bt // chunk
    halo = CONV_HALO

    @pl.when(pl.program_id(1) == 0)
    def _():
        state_ref[...] = jnp.zeros_like(state_ref)
        for buf in (qbuf, kbuf, vbuf):
            buf[0:halo, :] = jnp.zeros((halo, nh * dk), F32)

    qbuf[halo:halo + bt, :] = q_ref[...].astype(F32)
    kbuf[halo:halo + bt, :] = k_ref[...].astype(F32)
    vbuf[halo:halo + bt, :] = v_ref[...].astype(F32)

    row = lax.broadcasted_iota(jnp.int32, (chunk, chunk), 0)
    col = lax.broadcasted_iota(jnp.int32, (chunk, chunk), 1)
    incl = row >= col
    strict = row > col
    eye = (row == col).astype(F32)

    def conv_silu(buf, w_ref, lanes):
        w = w_ref[:, lanes]
        acc = buf[halo:halo + bt, lanes] * w[GDN_CONV - 1:GDN_CONV, :]
        for j in range(GDN_CONV - 1):
            sh = GDN_CONV - 1 - j
            acc = acc + buf[halo - sh:halo - sh + bt, lanes] * w[j:j + 1, :]
        return _silu(acc)

    def l2n(x, scale=1.0):
        return x * (lax.rsqrt(jnp.sum(x * x, axis=-1, keepdims=True) + RMS_EPS) * scale)

    def chunk_rows(ref, hh):
        tiles = [jnp.broadcast_to(ref[hh:hh + 1, c * chunk:(c + 1) * chunk], (chunk, chunk)) for c in range(nc)]
        return jnp.stack(tiles), jnp.stack([t.T for t in tiles])

    def phase_a(hh):
        lanes = slice(hh * dk, (hh + 1) * dk)
        q = l2n(conv_silu(qbuf, wq_ref, lanes), dk ** -0.5).reshape(nc, chunk, dk)
        k = l2n(conv_silu(kbuf, wk_ref, lanes)).reshape(nc, chunk, dk)
        v = conv_silu(vbuf, wv_ref, lanes).reshape(nc, chunk, dk)

        gc_rb, gc_cb = chunk_rows(gc_ref, hh)
        _, bt_cb = chunk_rows(beta_ref, hh)
        decay = jnp.exp(jnp.where(incl, gc_cb - gc_rb, -jnp.inf))

        qkk = jnp.einsum("nid,njd->nij", jnp.concatenate([q, k], axis=1).astype(BF16), k.astype(BF16),
                         preferred_element_type=F32)
        qk = qkk[:, :chunk] * decay
        a_mat = jnp.where(strict, qkk[:, chunk:] * bt_cb * decay, 0.0)
        t_mat = _unit_lower_inverse(a_mat, row, col, eye, GDN_INV_BASE)

        e_gc = jnp.exp(gc_cb)
        uw = _bmm(t_mat, jnp.concatenate([v * bt_cb, k * (bt_cb * e_gc)], axis=2))
        gl_row = gc_cb[:, chunk - 1:chunk, :]
        k_dec = k * jnp.exp(gl_row - gc_cb)

        u_s[:, hh] = uw[:, :, :dk]
        wq_s[:, hh] = jnp.concatenate([uw[:, :, dk:], q * e_gc], axis=1).astype(BF16)
        qk_s[:, hh] = qk.astype(BF16)
        kdt_s[:, hh] = jnp.stack([k_dec[c].T for c in range(nc)]).astype(BF16)
        gl_s[:, hh] = jnp.broadcast_to(jnp.exp(gl_row), (nc, 8, dk))

    for hh in range(nh):
        phase_a(hh)

    def phase_b(c, state):
        ws = _bmm(wq_s[c], state)
        v_new = u_s[c] - ws[:, :chunk]
        vb = v_new.astype(BF16)
        o = ws[:, chunk:] + _bmm(qk_s[c], vb)
        state = state * gl_s[c][:, 0:1, :] + _bmm(kdt_s[c], vb)
        on = o * lax.rsqrt(jnp.mean(o * o, axis=-1, keepdims=True) + RMS_EPS) * ng_ref[...]
        r0 = pl.multiple_of(c * chunk, chunk)
        z = z_ref[pl.ds(r0, chunk), :].astype(F32)
        for hh in range(nh):
            lanes = slice(hh * dk, (hh + 1) * dk)
            o_ref[pl.ds(r0, chunk), lanes] = (on[hh] * _silu(z[:, lanes])).astype(o_ref.dtype)
        return state

    state_ref[...] = lax.fori_loop(0, nc, phase_b, state_ref[...], unroll=True)
    for buf in (qbuf, kbuf, vbuf):
        buf[0:halo, :] = buf[bt:bt + halo, :]


def _gdn_core(proj, gc, beta, w_conv, norm_g, chunk=GDN_CHUNK, bt=GDN_BLOCK, nh=GDN_HEADS_PER_STEP):
    s = proj.shape[0]
    bt = min(bt, s)
    dk = HEAD_DIM
    nc = bt // chunk
    groups = HEADS // nh
    tok = lambda part: pl.BlockSpec((bt, nh * dk), lambda h, b: (b, part * groups + h))
    gate = pl.BlockSpec((None, nh, bt), lambda h, b: (h, 0, b))
    cw = lambda part: pl.BlockSpec((GDN_CONV, nh * dk), lambda h, b: (0, part * groups + h))
    return pl.pallas_call(
        functools.partial(_gdn_core_kernel, chunk=chunk),
        grid=(groups, s // bt),
        in_specs=[tok(0), tok(1), tok(2), tok(3), gate, gate,
                  cw(0), cw(1), cw(2), pl.BlockSpec((1, dk), lambda h, b: (0, 0))],
        out_specs=pl.BlockSpec((bt, nh * dk), lambda h, b: (b, h)),
        out_shape=jax.ShapeDtypeStruct((s, HEADS * dk), BF16),
        scratch_shapes=[pltpu.VMEM((nh, dk, dk), F32)] + [pltpu.VMEM((bt + 2 * CONV_HALO, nh * dk), F32)] * 3
        + [pltpu.VMEM((nc, nh, chunk, dk), F32), pltpu.VMEM((nc, nh, 2 * chunk, dk), BF16),
           pltpu.VMEM((nc, nh, chunk, chunk), BF16), pltpu.VMEM((nc, nh, dk, chunk), BF16),
           pltpu.VMEM((nc, nh, 8, dk), F32)],
        compiler_params=_params("parallel", "arbitrary"), name="gdn_core",
    )(proj, proj, proj, proj, gc.reshape(groups, nh, s), beta.reshape(groups, nh, s),
      w_conv, w_conv, w_conv, norm_g.reshape(1, dk))


def _rope_tab_kernel(pos_ref, invf_ref, cm_ref, sn_ref, sp_ref):
    ang = pos_ref[...].astype(F32) * invf_ref[...]
    cos = jnp.cos(ang)
    sin = jnp.sin(ang)
    lane = lax.broadcasted_iota(jnp.int32, ang.shape, 1)
    half = MLA_ROPE // 2
    cm_ref[...] = jnp.where(lane < MLA_ROPE, cos, 0.0)
    sn_ref[...] = jnp.where(lane < half, -sin, 0.0)
    sp_ref[...] = jnp.where((lane >= half) & (lane < MLA_ROPE), sin, 0.0)


def _rope_tables(positions, bm=512):
    s = positions.shape[0]
    bm = min(bm, s)
    half = MLA_ROPE // 2
    inv_freq = ROPE_THETA ** (-jnp.arange(0, half, dtype=F32) / half)
    invf = jnp.concatenate([inv_freq, inv_freq, jnp.zeros((HEAD_DIM - MLA_ROPE,), F32)]).reshape(1, HEAD_DIM)
    tab = pl.BlockSpec((bm, HEAD_DIM), lambda i: (i, 0))
    return pl.pallas_call(
        _rope_tab_kernel, grid=(s // bm,),
        in_specs=[pl.BlockSpec((bm, 1), lambda i: (i, 0)), pl.BlockSpec((1, HEAD_DIM), lambda i: (0, 0))],
        out_specs=[tab, tab, tab],
        out_shape=[jax.ShapeDtypeStruct((s, HEAD_DIM), F32)] * 3,
        compiler_params=_params("parallel"), name="rope_tables",
    )(positions.reshape(s, 1), invf)


def _rope(x, cm, sn, sp):
    half = MLA_ROPE // 2
    return x * cm + pltpu.roll(x, HEAD_DIM - half, axis=1) * sn + pltpu.roll(x, half, axis=1) * sp


def _q_proj_kernel(a_ref, g_ref, w_ref, cm_ref, sn_ref, sp_ref, o_ref):
    an = _rms_rows(a_ref[...], g_ref[...]).astype(BF16)
    acc = jnp.dot(an, w_ref[...], preferred_element_type=F32)
    scale = MLA_QK ** -0.5 * LOG2_E
    cm, sn, sp = cm_ref[...] * scale, sn_ref[...] * scale, sp_ref[...] * scale
    for g in range(o_ref.shape[1] // MLA_QK_PAD):
        lo = g * MLA_QK_PAD
        hi = lo + HEAD_DIM
        o_ref[:, lo:hi] = (acc[:, lo:hi] * scale).astype(o_ref.dtype)
        o_ref[:, hi:hi + HEAD_DIM] = _rope(acc[:, hi:hi + HEAD_DIM], cm, sn, sp).astype(o_ref.dtype)


def _kv_proj_kernel(a_ref, g_ref, wk_ref, wv_ref, kr_ref, cm_ref, sn_ref, sp_ref, k_ref, v_ref):
    an = _rms_rows(a_ref[...], g_ref[...]).astype(BF16)
    v_ref[...] = jnp.dot(an, wv_ref[...], preferred_element_type=F32).astype(v_ref.dtype)
    acc = jnp.dot(an, wk_ref[...], preferred_element_type=F32)
    kr = _rope(kr_ref[...], cm_ref[...], sn_ref[...], sp_ref[...]).astype(k_ref.dtype)
    for g in range(k_ref.shape[1] // MLA_QK_PAD):
        lo = g * MLA_QK_PAD
        k_ref[:, lo:lo + HEAD_DIM] = acc[:, g * HEAD_DIM:(g + 1) * HEAD_DIM].astype(k_ref.dtype)
        k_ref[:, lo + HEAD_DIM:lo + MLA_QK_PAD] = kr


def _q_proj(a, g, w, tables, bm=1024, heads_per_step=8):
    m, k = a.shape
    bm = min(bm, m)
    bn = heads_per_step * MLA_QK_PAD
    tab = pl.BlockSpec((bm, HEAD_DIM), lambda j, i: (i, 0))
    return pl.pallas_call(
        _q_proj_kernel, grid=(HEADS // heads_per_step, m // bm),
        in_specs=[pl.BlockSpec((bm, k), lambda j, i: (i, 0)),
                  pl.BlockSpec((1, k), lambda j, i: (0, 0)),
                  pl.BlockSpec((k, bn), lambda j, i: (0, j))] + [tab] * len(tables),
        out_specs=pl.BlockSpec((bm, bn), lambda j, i: (i, j)),
        out_shape=jax.ShapeDtypeStruct((m, HEADS * MLA_QK_PAD), BF16),
        compiler_params=_params("parallel", "parallel"), name="q_proj",
    )(a, g, w, *tables)


def _kv_proj(a, g, wk, wv, tables, bm=1024, heads_per_step=8):
    m, k = a.shape
    bm = min(bm, m)
    bw = heads_per_step * HEAD_DIM
    tab = pl.BlockSpec((bm, HEAD_DIM), lambda j, i: (i, 0))
    wspec = pl.BlockSpec((k, bw), lambda j, i: (0, j))
    return pl.pallas_call(
        _kv_proj_kernel, grid=(HEADS // heads_per_step, m // bm),
        in_specs=[pl.BlockSpec((bm, k), lambda j, i: (i, 0)),
                  pl.BlockSpec((1, k), lambda j, i: (0, 0)), wspec, wspec] + [tab] * len(tables),
        out_specs=[pl.BlockSpec((bm, heads_per_step * MLA_QK_PAD), lambda j, i: (i, j)),
                   pl.BlockSpec((bm, bw), lambda j, i: (i, j))],
        out_shape=[jax.ShapeDtypeStruct((m, HEADS * MLA_QK_PAD), BF16),
                   jax.ShapeDtypeStruct((m, HEADS * HEAD_DIM), BF16)],
        compiler_params=_params("parallel", "parallel"), name="kv_proj",
    )(a, g, wk, wv, *tables)


def _latent_kernel(a_ref, w_ref, cq_ref, ckv_ref, kr_ref):
    acc = _dot(a_ref[...], w_ref[...], True)
    cq_ref[...] = acc[:, :MLA_Q_RANK]
    ckv_ref[...] = acc[:, MLA_Q_RANK:MLA_Q_RANK + MLA_KV_RANK]
    kr_ref[...] = acc[:, MLA_Q_RANK + MLA_KV_RANK:]


def _latent_proj(h, w_lat, bm=512):
    m, k = h.shape
    n = w_lat.shape[0]
    bm = min(bm, m)
    widths = (MLA_Q_RANK, MLA_KV_RANK, n - MLA_Q_RANK - MLA_KV_RANK)
    return pl.pallas_call(
        _latent_kernel, grid=(m // bm,),
        in_specs=[pl.BlockSpec((bm, k), lambda i: (i, 0)), pl.BlockSpec((n, k), lambda i: (0, 0))],
        out_specs=[pl.BlockSpec((bm, wd), lambda i: (i, 0)) for wd in widths],
        out_shape=[jax.ShapeDtypeStruct((m, wd), F32) for wd in widths],
        compiler_params=_params("parallel"), name="latent_proj",
    )(h, w_lat)


def _attn_kernel(q_ref, k_ref, v_ref, z_ref, o_ref, sa_ref, sb_ref, m_ref, acc_ref, *, tk):
    i = pl.program_id(1)
    tq = q_ref.shape[0]
    dv = v_ref.shape[1]
    rows = tq // ATT_GROUPS

    def scores(s_ref, blk):
        k_blk = k_ref[pl.ds(pl.multiple_of(blk * tk, tk), tk), :]
        s_ref[...] = lax.dot_general(q_ref[...], k_blk, (((1,), (1,)), ((), ())), preferred_element_type=F32)

    def softmax_pv(s_ref, blk, masked):
        start = pl.multiple_of(blk * tk, tk)
        for g in range(ATT_GROUPS):
            width = (g + 1) * rows if masked else tk
            tiles = width // dv
            s = s_ref[g * rows:(g + 1) * rows, 0:width]
            if masked:
                row = lax.broadcasted_iota(jnp.int32, (rows, width), 0) + g * rows
                col = lax.broadcasted_iota(jnp.int32, (rows, width), 1)
                s = jnp.where(col <= row, s, -jnp.inf)
            m_part = s[:, :dv]
            for t in range(1, tiles):
                m_part = jnp.maximum(m_part, s[:, t * dv:(t + 1) * dv])
            m_prev = m_ref[g]
            m_new = jnp.maximum(m_prev, jnp.broadcast_to(jnp.max(m_part, axis=-1, keepdims=True), (rows, dv)))
            alpha = jnp.exp2(m_prev - m_new)
            p = jnp.exp2(s - jnp.concatenate([m_new] * tiles, axis=1)).astype(BF16)
            v_ext = jnp.concatenate([v_ref[pl.ds(start, width), :], jnp.ones((width, dv), BF16)], axis=1)
            acc_ref[g] = (jnp.concatenate([alpha, alpha], axis=1) * acc_ref[g]
                          + jnp.dot(p, v_ext, preferred_element_type=F32))
            m_ref[g] = m_new

    def finish():
        for g in range(ATT_GROUPS):
            acc = acc_ref[g]
            o = acc[:, :dv] / acc[:, dv:]
            z = z_ref[g * rows:(g + 1) * rows, :].astype(F32)
            o_ref[g * rows:(g + 1) * rows, :] = (o * _silu(z)).astype(o_ref.dtype)

    m_ref[...] = jnp.full(m_ref.shape, -jnp.inf, F32)
    acc_ref[...] = jnp.zeros(acc_ref.shape, F32)
    scores(sa_ref, 0)

    def pair(jj, carry):
        blk = 2 * jj
        scores(sb_ref, blk + 1)
        softmax_pv(sa_ref, blk, False)
        scores(sa_ref, blk + 2)
        softmax_pv(sb_ref, blk + 1, False)
        return carry

    lax.fori_loop(0, i // 2, pair, 0)

    @pl.when(i % 2 == 0)
    def _():
        softmax_pv(sa_ref, i, True)
        finish()

    @pl.when(i % 2 == 1)
    def _():
        scores(sb_ref, i)
        softmax_pv(sa_ref, i - 1, False)
        softmax_pv(sb_ref, i, True)
        finish()


def _attention(q, k, v, z, tq=ATT_TQ):
    s = q.shape[0]
    tq = min(tq, s)
    tk = tq
    rows = tq // ATT_GROUPS
    return pl.pallas_call(
        functools.partial(_attn_kernel, tk=tk), grid=(HEADS, s // tq),
        in_specs=[pl.BlockSpec((tq, MLA_QK_PAD), lambda h, i: (i, h)),
                  pl.BlockSpec((s, MLA_QK_PAD), lambda h, i: (0, h)),
                  pl.BlockSpec((s, HEAD_DIM), lambda h, i: (0, h)),
                  pl.BlockSpec((tq, HEAD_DIM), lambda h, i: (i, h))],
        out_specs=pl.BlockSpec((tq, HEAD_DIM), lambda h, i: (i, h)),
        out_shape=jax.ShapeDtypeStruct((s, HEADS * HEAD_DIM), BF16),
        scratch_shapes=[pltpu.VMEM((tq, tk), F32), pltpu.VMEM((tq, tk), F32),
                        pltpu.VMEM((ATT_GROUPS, rows, HEAD_DIM), F32),
                        pltpu.VMEM((ATT_GROUPS, rows, 2 * HEAD_DIM), F32)],
        compiler_params=_params("parallel", "arbitrary"), name="mla_attention",
    )(q, k, v, z)


def _gdn_branch(h, w_in, w_conv, a_log, dt_bias, norm_g, w_out):
    width = 4 * HEADS * HEAD_DIM
    w_in_t = w_in.T
    proj = _matmul_w32(h, w_in_t, width, BF16, *IN_PROJ_BLOCK, nt=True)
    w_gate = jnp.pad(w_in_t[width:], ((0, 128 - 2 * HEADS), (0, 0))).astype(BF16)
    ab = _matmul(h, w_gate, F32, *GATE_PROJ_BLOCK, nt=True)[:, :2 * HEADS]
    beta, gc = _gdn_gates(ab.T, a_log, dt_bias, GDN_CHUNK)
    gated = _gdn_core(proj, gc, beta, w_conv, norm_g)
    return _matmul_w32(gated, w_out, w_out.shape[1], BF16, *OUT_PROJ_BLOCK)


def _mla_branch(h, positions, w_in, q_norm_g, w_qb, kv_norm_g, w_kvb, w_out):
    s = h.shape[0]
    n_lat = MLA_Q_RANK + MLA_KV_RANK + MLA_ROPE
    w_in_t = w_in.T
    w_lat = jnp.pad(w_in_t[:n_lat], ((0, HEAD_DIM - MLA_ROPE), (0, 0))).astype(BF16)
    cq, ckv, k_rope = _latent_proj(h, w_lat)
    z = _matmul(h, w_in_t[n_lat:].astype(BF16), BF16, *Z_PROJ_BLOCK, nt=True)

    wq = w_qb.reshape(MLA_Q_RANK, HEADS, MLA_QK)
    wq = jnp.pad(wq, ((0, 0), (0, 0), (0, MLA_QK_PAD - MLA_QK))).reshape(MLA_Q_RANK, HEADS * MLA_QK_PAD)
    wkv = w_kvb.reshape(MLA_KV_RANK, HEADS, 2 * HEAD_DIM)
    wk = wkv[:, :, :HEAD_DIM].reshape(MLA_KV_RANK, HEADS * HEAD_DIM)
    wv = wkv[:, :, HEAD_DIM:].reshape(MLA_KV_RANK, HEADS * HEAD_DIM)

    cm, sn, sp = _rope_tables(positions.reshape(s))
    q = _q_proj(cq, q_norm_g.reshape(1, -1), wq.astype(BF16), (cm, sn, sp))
    k, v = _kv_proj(ckv, kv_norm_g.reshape(1, -1), wk.astype(BF16), wv.astype(BF16), (k_rope, cm, sn, sp))
    gated = _attention(q, k, v, z)
    return _matmul_w32(gated, w_out, w_out.shape[1], BF16, *OUT_PROJ_BLOCK)


def kernel(x, c, positions, w_mod, b_mod, ln_g, ln_b, a_w_in, a_w_conv, a_a_log, a_dt_bias, a_norm_g, a_w_out,
           b_w_in, b_q_norm_g, b_w_qb, b_kv_norm_g, b_w_kvb, b_w_out):
    bsz, s, d = x.shape
    assert bsz == 1 and d == D_MODEL and DEPTH == 2
    x0 = x.reshape(s, d)
    mod = _modulation(c, w_mod, b_mod)
    vec = lambda t: t.reshape(1, d)

    h0 = _modulate(x0, mod[0, 1], mod[0, 0])
    y0 = _gdn_branch(h0, a_w_in[0], a_w_conv[0], a_a_log[0], a_dt_bias[0], a_norm_g[0], a_w_out[0])
    x1, h1 = _residual_layernorm(x0, y0, mod[0, 2], vec(ln_g[0]), vec(ln_b[0]), mod[1, 1], mod[1, 0])
    y1 = _mla_branch(h1, positions, b_w_in[0], b_q_norm_g[0], b_w_qb[0], b_kv_norm_g[0], b_w_kvb[0], b_w_out[0])
    x2 = _residual_layernorm(x1, y1, mod[1, 2], vec(ln_g[1]), vec(ln_b[1]))
    return x2.reshape(bsz, s, d)
```
